```python
import jax
import jax.numpy as jnp
from jax import lax
import numpy as np

D_MODEL = 1024
BATCH = 8
SEQ = 2048
DEPTH = 4

GRID_W = 64
CTX_LEN = 256
EPS = 1e-6
NEG_INF = -1e30
ROPE_THETA = 10000.0

N_BRANCH = 3
BRANCH_WIDTH = 512

GDN_HEADS = 4
GDN_DK = 128
GDN_DV = 128
GDN_CHUNK = 64
GDN_CONV = 4

SWA_Q_HEADS = 8
SWA_KV_HEADS = 2
SWA_GROUP = SWA_Q_HEADS // SWA_KV_HEADS
SWA_HEAD_DIM = 64
SWA_WINDOW = 128
SWA_BLOCK = 128

LRU_WIDTH = 512
LRU_BLOCKS = 8
LRU_CONV = 4
LRU_C = 8.0

N_EXPERTS = 16
EXPERT_HIDDEN = 2048
EC_CAPACITY = 2

IN_SIZES = (GDN_HEADS * GDN_DK, GDN_HEADS * GDN_DK, GDN_HEADS * GDN_DV, GDN_HEADS * GDN_DV,
            2 * GDN_HEADS, 2 * GDN_HEADS,
            SWA_Q_HEADS * SWA_HEAD_DIM, SWA_KV_HEADS * SWA_HEAD_DIM, SWA_KV_HEADS * SWA_HEAD_DIM,
            LRU_WIDTH, LRU_WIDTH,
            N_BRANCH * D_MODEL)
IN_SPLITS = tuple(sum(IN_SIZES[:i + 1]) for i in range(len(IN_SIZES) - 1))
D_IN = sum(IN_SIZES)

kernel_name = 'hybrid_gdn_swa_rglru_ecmoe_prefix_dit'

F32 = jnp.float32


def rmsnorm(t, g):
    tf = t.astype(F32)
    tf = tf * lax.rsqrt(jnp.mean(tf * tf, axis=-1, keepdims=True) + EPS)
    return (tf * g.astype(F32)).astype(t.dtype)


def l2norm(t):
    tf = t.astype(F32)
    return tf * lax.rsqrt(jnp.sum(tf * tf, axis=-1, keepdims=True) + EPS)


def split_heads(t, n):
    return t.reshape(t.shape[:-1] + (n, t.shape[-1] // n))


def maybe_flip(t, rev):
    return jnp.flip(t, axis=1) if rev else t


def centred_depthwise_conv(t, w):
    k = w.shape[0]
    return lax.conv_general_dilated(
        t, w[:, None, :].astype(t.dtype), window_strides=(1,),
        padding=[(k // 2, k - 1 - k // 2)], dimension_numbers=('NWC', 'WIO', 'NWC'),
        feature_group_count=t.shape[-1])


def adaln(cond, w, b):
    return jnp.split(jax.nn.silu(cond) @ w + b, 6, axis=-1)


def modulate(t, shift, scale):
    return t * (1 + scale) + shift


def unit_lower_inverse(a):
    c = a.shape[-1]
    eye = jnp.eye(c, dtype=a.dtype)
    inv = eye - a
    p = a
    for _ in range(int(np.log2(c)) - 1):
        p = p @ p
        inv = inv @ (eye + p)
    return inv


def gdn_chunked(q, k, v, g, beta, s0):
    b, l, h, _ = q.shape
    c = GDN_CHUNK
    n = l // c

    def blocks(t):
        return jnp.moveaxis(t.reshape((b, n, c, h) + t.shape[3:]), (1, 2), (0, 3))

    qb, kb, vb, betab = blocks(q), blocks(k), blocks(v), blocks(beta)
    gb = jnp.cumsum(blocks(g), axis=-1)
    incl = jnp.tril(jnp.ones((c, c), bool))
    strict = jnp.tril(jnp.ones((c, c), bool), -1)
    diff = gb[..., :, None] - gb[..., None, :]
    decay = jnp.where(incl, jnp.exp(jnp.where(incl, diff, 0.0)), 0.0)
    k_beta = kb * betab[..., None]
    a_mat = jnp.where(strict, jnp.einsum('nbhid,nbhjd->nbhij', k_beta, kb) * decay, 0.0)
    t_mat = unit_lower_inverse(a_mat)
    u = t_mat @ (vb * betab[..., None])
    w = t_mat @ (k_beta * jnp.exp(gb)[..., None])
    qk = jnp.einsum('nbhid,nbhjd->nbhij', qb, kb) * decay

    def step(s, inp):
        q_c, k_c, u_c, w_c, qk_c, g_c = inp
        v_new = u_c - w_c @ s
        o = (q_c * jnp.exp(g_c)[..., None]) @ s + qk_c @ v_new
        g_last = g_c[..., -1:]
        s = s * jnp.exp(g_last)[..., None] + jnp.einsum(
            'bhcd,bhce->bhde', k_c * jnp.exp(g_last - g_c)[..., None], v_new)
        return s, o

    s_fin, o = lax.scan(step, s0, (qb, kb, u, w, qk, gb))
    o = jnp.moveaxis(o, (0, 3), (1, 2)).reshape(b, l, h, v.shape[-1])
    return o, s_fin


def gdn_prepare(q, k, v, conv_w):
    qkv = jax.nn.silu(centred_depthwise_conv(jnp.concatenate([q, k, v], axis=-1), conv_w))
    q, k, v = jnp.split(qkv, [GDN_HEADS * GDN_DK, 2 * GDN_HEADS * GDN_DK], axis=-1)
    q = l2norm(split_heads(q, GDN_HEADS)) * (GDN_DK ** -0.5)
    k = l2norm(split_heads(k, GDN_HEADS))
    v = split_heads(v, GDN_HEADS).astype(F32)
    return q, k, v


def gdn_gates(a, b, a_log, dt_bias, d):
    sl = slice(d * GDN_HEADS, (d + 1) * GDN_HEADS)
    g = -jnp.exp(a_log[d].astype(F32)) * jax.nn.softplus(a[..., sl].astype(F32) + dt_bias[d].astype(F32))
    beta = jax.nn.sigmoid(b[..., sl].astype(F32))
    return g, beta


def gdn_mixer(parts_x, parts_c, conv_w, a_log, dt_bias, onorm_g, need_ctx):
    qx, kx, vx, zx, ax, bx = parts_x
    qc, kc, vc, zc, ac, bc = parts_c
    seq_x = gdn_prepare(qx, kx, vx, conv_w)
    seq_c = gdn_prepare(qc, kc, vc, conv_w)
    batch = qx.shape[0]
    o_x = jnp.zeros(seq_x[2].shape, F32)
    o_c = jnp.zeros(seq_c[2].shape, F32)
    for d in range(2):
        rev = d == 1
        gx, betax = gdn_gates(ax, bx, a_log, dt_bias, d)
        gc, betac = gdn_gates(ac, bc, a_log, dt_bias, d)
        s0 = jnp.zeros((batch, GDN_HEADS, GDN_DK, GDN_DV), F32)
        oc_d, s_ctx = gdn_chunked(*(maybe_flip(t, rev) for t in (*seq_c, gc, betac)), s0)
        ox_d, _ = gdn_chunked(*(maybe_flip(t, rev) for t in (*seq_x, gx, betax)), s_ctx)
        o_x = o_x + maybe_flip(ox_d, rev)
        o_c = o_c + maybe_flip(oc_d, rev)

    def finish(o, z):
        y = rmsnorm(o, onorm_g) * jax.nn.silu(split_heads(z, GDN_HEADS).astype(F32))
        return y.reshape(z.shape).astype(z.dtype)

    return finish(o_x, zx), (finish(o_c, zc) if need_ctx else None)


def axial_rope(length, dh):
    rows = length // GRID_W
    row = jnp.repeat(jnp.arange(rows, dtype=F32), GRID_W)
    col = jnp.tile(jnp.arange(GRID_W, dtype=F32), rows)
    axis_dim = dh // 2
    inv_freq = ROPE_THETA ** (-jnp.arange(0, axis_dim, 2, dtype=F32) / axis_dim)
    ang = jnp.concatenate([row[:, None] * inv_freq, col[:, None] * inv_freq], axis=-1)
    return jnp.cos(ang), jnp.sin(ang)


def apply_rope(t, cos, sin):
    tf = t.astype(F32)
    t1, t2 = tf[..., 0::2], tf[..., 1::2]
    cs, sn = cos[None, :, None, :], sin[None, :, None, :]
    out = jnp.stack([t1 * cs - t2 * sn, t1 * sn + t2 * cs], axis=-1).reshape(t.shape)
    return out.astype(t.dtype)


def band_mask(nb, length):
    a = jnp.arange(SWA_BLOCK)[:, None]
    s = jnp.arange(3 * SWA_BLOCK)[None, :]
    kpos = jnp.arange(nb)[:, None, None] * SWA_BLOCK - SWA_BLOCK + s
    in_window = jnp.abs(s - SWA_BLOCK - a) <= SWA_WINDOW
    return in_window[None] & (kpos >= 0) & (kpos < length)


def swa_mixer(qx, kx, vx, qc, kc, vc, qn_g, kn_g, sink, need_ctx):
    b, l, _ = qx.shape
    lc = qc.shape[1]
    scale = SWA_HEAD_DIM ** -0.5
    cos, sin = axial_rope(l, SWA_HEAD_DIM)
    qx = apply_rope(rmsnorm(split_heads(qx, SWA_Q_HEADS), qn_g), cos, sin)
    kx = apply_rope(rmsnorm(split_heads(kx, SWA_KV_HEADS), kn_g), cos, sin)
    vx = split_heads(vx, SWA_KV_HEADS)
    qc = rmsnorm(split_heads(qc, SWA_Q_HEADS), qn_g)
    kc = rmsnorm(split_heads(kc, SWA_KV_HEADS), kn_g)
    vc = split_heads(vc, SWA_KV_HEADS)
    sink_l = sink.astype(F32).reshape(SWA_KV_HEADS, SWA_GROUP)

    nb = l // SWA_BLOCK
    qb = qx.reshape(b, nb, SWA_BLOCK, SWA_KV_HEADS, SWA_GROUP, SWA_HEAD_DIM)

    def windows(t):
        tp = jnp.pad(t, ((0, 0), (SWA_BLOCK, SWA_BLOCK), (0, 0), (0, 0)))
        tp = tp.reshape(b, nb + 2, SWA_BLOCK, SWA_KV_HEADS, SWA_HEAD_DIM)
        return jnp.concatenate([tp[:, :-2], tp[:, 1:-1], tp[:, 2:]], axis=2)

    kw, vw = windows(kx), windows(vx)
    s_loc = jnp.einsum('bnqkgd,bnskd->bnkgqs', qb, kw).astype(F32) * scale
    s_loc = jnp.where(band_mask(nb, l)[None, :, None, None], s_loc, NEG_INF)
    s_ctx = jnp.einsum('bnqkgd,bskd->bnkgqs', qb, kc).astype(F32) * scale
    sink_col = jnp.broadcast_to(sink_l[None, None, :, :, None, None], s_loc.shape[:-1] + (1,))
    p = jax.nn.softmax(jnp.concatenate([s_loc, s_ctx, sink_col], axis=-1), axis=-1).astype(vx.dtype)
    w3 = 3 * SWA_BLOCK
    o = (jnp.einsum('bnkgqs,bnskd->bnqkgd', p[..., :w3], vw)
         + jnp.einsum('bnkgqs,bskd->bnqkgd', p[..., w3:w3 + lc], vc))
    out_x = o.reshape(b, l, SWA_Q_HEADS * SWA_HEAD_DIM)
    if not need_ctx:
        return out_x, None

    qcg = qc.reshape(b, lc, SWA_KV_HEADS, SWA_GROUP, SWA_HEAD_DIM)
    s = jnp.einsum('bqkgd,bskd->bkgqs', qcg, kc).astype(F32) * scale
    sink_c = jnp.broadcast_to(sink_l[None, :, :, None, None], s.shape[:-1] + (1,))
    pc = jax.nn.softmax(jnp.concatenate([s, sink_c], axis=-1), axis=-1)[..., :lc].astype(vc.dtype)
    out_c = jnp.einsum('bkgqs,bskd->bqkgd', pc, vc).reshape(b, lc, SWA_Q_HEADS * SWA_HEAD_DIM)
    return out_x, out_c


def linear_scan(a, bterm, h0):
    bterm = bterm.at[:, 0].add(a[:, 0] * h0)

    def combine(lft, rgt):
        return lft[0] * rgt[0], rgt[0] * lft[1] + rgt[1]

    _, h = lax.associative_scan(combine, (a, bterm), axis=1)
    return h


def lru_coeffs(u, gate_w, gate_b, lam):
    b, l, _ = u.shape
    ub = u.reshape(b, l, LRU_BLOCKS, LRU_WIDTH // LRU_BLOCKS)
    pre = jnp.einsum('blki,gkij->gblkj', ub, gate_w.astype(F32)).reshape(2, b, l, LRU_WIDTH)
    pre = pre + gate_b.astype(F32)[:, None, None, :]
    r, i = jax.nn.sigmoid(pre[0]), jax.nn.sigmoid(pre[1])
    log_a = -LRU_C * r * jax.nn.softplus(-lam.astype(F32))
    return jnp.exp(log_a), jnp.sqrt(-jnp.expm1(2.0 * log_a)) * (i * u)


def lru_mixer(xr_x, y_x, xr_c, y_c, conv_w, conv_b, gate_w, gate_b, lam, need_ctx):
    ux = (centred_depthwise_conv(xr_x, conv_w) + conv_b).astype(F32)
    uc = (centred_depthwise_conv(xr_c, conv_w) + conv_b).astype(F32)
    batch = ux.shape[0]
    h_x = jnp.zeros(ux.shape, F32)
    h_c = jnp.zeros(uc.shape, F32)
    for d in range(2):
        rev = d == 1
        ac, bc = lru_coeffs(maybe_flip(uc, rev), gate_w[d], gate_b[d], lam[d])
        ax, bx = lru_coeffs(maybe_flip(ux, rev), gate_w[d], gate_b[d], lam[d])
        hc_d = linear_scan(ac, bc, jnp.zeros((batch, LRU_WIDTH), F32))
        hx_d = linear_scan(ax, bx, hc_d[:, -1])
        h_x = h_x + maybe_flip(hx_d, rev)
        h_c = h_c + maybe_flip(hc_d, rev)
    out_x = (h_x * jax.nn.gelu(y_x.astype(F32))).astype(y_x.dtype)
    out_c = (h_c * jax.nn.gelu(y_c.astype(F32))).astype(y_c.dtype) if need_ctx else None
    return out_x, out_c


def merge(branches, gate_logits, w_branch, w_out):
    gates = jnp.split(jax.nn.sigmoid(gate_logits.astype(F32)).astype(gate_logits.dtype), N_BRANCH, axis=-1)
    mixed = gates[0] * (branches[0] @ w_branch[0])
    for i in range(1, N_BRANCH):
        mixed = mixed + gates[i] * (branches[i] @ w_branch[i])
    return mixed @ w_out


def mixer_sublayer(hx, hc, w_in, gdn_conv_w, gdn_a_log, gdn_dt_bias, gdn_onorm_g,
                   swa_qnorm_g, swa_knorm_g, swa_sink, lru_conv_w, lru_conv_b,
                   lru_gate_w, lru_gate_b, lru_lambda, w_branch, w_out, need_ctx):
    px = jnp.split(hx @ w_in, IN_SPLITS, axis=-1)
    pc = jnp.split(hc @ w_in, IN_SPLITS, axis=-1)
    a_x, a_c = gdn_mixer(px[0:6], pc[0:6], gdn_conv_w, gdn_a_log, gdn_dt_bias, gdn_onorm_g, need_ctx)
    b_x, b_c = swa_mixer(*px[6:9], *pc[6:9], swa_qnorm_g, swa_knorm_g, swa_sink, need_ctx)
    c_x, c_c = lru_mixer(px[9], px[10], pc[9], pc[10], lru_conv_w, lru_conv_b,
                         lru_gate_w, lru_gate_b, lru_lambda, need_ctx)
    out_x = merge((a_x, b_x, c_x), px[11], w_branch, w_out)
    out_c = merge((a_c, b_c, c_c), pc[11], w_branch, w_out) if need_ctx else None
    return out_x, out_c


def expert_choice_ffn(h, w_router, w_gate, w_up, w_down):
    b, t, _ = h.shape
    cap = EC_CAPACITY * t // N_EXPERTS
    aff = jax.nn.softmax((h @ w_router).astype(F32), axis=-1)
    gates, idx = lax.top_k(jnp.swapaxes(aff, 1, 2), cap)
    bidx = jnp.arange(b)[:, None, None]
    xs = h[bidx, idx]
    hid = jax.nn.silu(jnp.einsum('becd,edf->becf', xs, w_gate)) * jnp.einsum('becd,edf->becf', xs, w_up)
    ys = jnp.einsum('becf,efd->becd', hid, w_down) * gates[..., None].astype(h.dtype)
    return jnp.zeros_like(h).at[bidx, idx].add(ys.astype(h.dtype))


def setup_inputs(seed: int = 0) -> dict:
    key = jax.random.key(seed)
    ks = jax.random.split(key, 32)

    def nrm(i, shape, scale):
        return jax.random.normal(ks[i], shape, F32) * scale

    dt = jnp.exp(jax.random.uniform(ks[9], (DEPTH, 2, GDN_HEADS), F32, np.log(1e-3), np.log(1e-1)))
    u = jax.random.uniform(ks[18], (DEPTH, 2, LRU_WIDTH), F32, 0.9, 0.999)
    a0 = u ** (1.0 / LRU_C)
    bw = LRU_WIDTH // LRU_BLOCKS
    return {
        'x': nrm(0, (BATCH, SEQ, D_MODEL), 1.0),
        'c': nrm(1, (BATCH, D_MODEL), 1.0),
        'ctx': nrm(2, (BATCH, CTX_LEN, D_MODEL), 1.0),
        'c_ctx': nrm(3, (D_MODEL,), 1.0),
        'w_ada': nrm(4, (DEPTH, D_MODEL, 6 * D_MODEL), 0.5 * D_MODEL ** -0.5),
        'b_ada': nrm(5, (DEPTH, 6 * D_MODEL), 0.01),
        'norm1_g': 1.0 + nrm(6, (DEPTH, D_MODEL), 0.01),
        'norm2_g': 1.0 + nrm(7, (DEPTH, D_MODEL), 0.01),
        'w_in': nrm(8, (DEPTH, D_MODEL, D_IN), D_MODEL ** -0.5),
        'gdn_conv_w': nrm(10, (DEPTH, GDN_CONV, 2 * GDN_HEADS * GDN_DK + GDN_HEADS * GDN_DV), GDN_CONV ** -0.5),
        'gdn_a_log': jnp.log(jax.random.uniform(ks[11], (DEPTH, 2, GDN_HEADS), F32, 1.0, 16.0)),
        'gdn_dt_bias': dt + jnp.log(-jnp.expm1(-dt)),
        'gdn_onorm_g': 1.0 + nrm(12, (DEPTH, GDN_DV), 0.01),
        'swa_qnorm_g': 1.0 + nrm(13, (DEPTH, SWA_HEAD_DIM), 0.01),
        'swa_knorm_g': 1.0 + nrm(14, (DEPTH, SWA_HEAD_DIM), 0.01),
        'swa_sink': nrm(15, (DEPTH, SWA_Q_HEADS), 0.5),
        'lru_conv_w': nrm(16, (DEPTH, LRU_CONV, LRU_WIDTH), LRU_CONV ** -0.5),
        'lru_conv_b': nrm(17, (DEPTH, LRU_WIDTH), 0.01),
        'lru_gate_w': nrm(19, (DEPTH, 2, 2, LRU_BLOCKS, bw, bw), bw ** -0.5),
        'lru_gate_b': nrm(20, (DEPTH, 2, 2, LRU_WIDTH), 0.01),
        'lru_lambda': jnp.log(a0) - jnp.log1p(-a0),
        'w_branch': nrm(21, (DEPTH, N_BRANCH, BRANCH_WIDTH, D_MODEL), BRANCH_WIDTH ** -0.5),
        'w_out': nrm(22, (DEPTH, D_MODEL, D_MODEL), D_MODEL ** -0.5),
        'w_router': nrm(23, (DEPTH, D_MODEL, N_EXPERTS), D_MODEL ** -0.5),
        'w_exp_gate': nrm(24, (DEPTH, N_EXPERTS, D_MODEL, EXPERT_HIDDEN), D_MODEL ** -0.5),
        'w_exp_up': nrm(25, (DEPTH, N_EXPERTS, D_MODEL, EXPERT_HIDDEN), D_MODEL ** -0.5),
        'w_exp_down': nrm(26, (DEPTH, N_EXPERTS, EXPERT_HIDDEN, D_MODEL), EXPERT_HIDDEN ** -0.5),
    }


def reference(x, c, ctx, c_ctx, w_ada, b_ada, norm1_g, norm2_g, w_in, gdn_conv_w, gdn_a_log,
              gdn_dt_bias, gdn_onorm_g, swa_qnorm_g, swa_knorm_g, swa_sink, lru_conv_w,
              lru_conv_b, lru_gate_w, lru_gate_b, lru_lambda, w_branch, w_out, w_router,
              w_exp_gate, w_exp_up, w_exp_down):
    cond_x = c[:, None, :]
    cond_c = c_ctx[None, None, :]
    cx = ctx
    for layer in range(DEPTH):
        need_ctx = layer < DEPTH - 1
        mx = adaln(cond_x, w_ada[layer], b_ada[layer])
        mc = adaln(cond_c, w_ada[layer], b_ada[layer])
        hx = modulate(rmsnorm(x, norm1_g[layer]), mx[0], mx[1])
        hc = modulate(rmsnorm(cx, norm1_g[layer]), mc[0], mc[1])
        ox, oc = mixer_sublayer(hx, hc, w_in[layer], gdn_conv_w[layer], gdn_a_log[layer],
                                gdn_dt_bias[layer], gdn_onorm_g[layer], swa_qnorm_g[layer],
                                swa_knorm_g[layer], swa_sink[layer], lru_conv_w[layer],
                                lru_conv_b[layer], lru_gate_w[layer], lru_gate_b[layer],
                                lru_lambda[layer], w_branch[layer], w_out[layer], need_ctx)
        x = x + mx[2] * ox
        hx = modulate(rmsnorm(x, norm2_g[layer]), mx[3], mx[4])
        x = x + mx[5] * expert_choice_ffn(hx, w_router[layer], w_exp_gate[layer],
                                          w_exp_up[layer], w_exp_down[layer])
        if need_ctx:
            cx = cx + mc[2] * oc
            hc = modulate(rmsnorm(cx, norm2_g[layer]), mc[3], mc[4])
            cx = cx + mc[5] * expert_choice_ffn(hc, w_router[layer], w_exp_gate[layer],
                                                w_exp_up[layer], w_exp_down[layer])
    return x
```

```python
import functools

import jax
import jax.numpy as jnp
import numpy as np
from jax import lax
from jax.experimental import pallas as pl
from jax.experimental.pallas import tpu as pltpu

F32 = jnp.float32
BF16 = jnp.bfloat16
I32 = jnp.int32

D = 1024
SEQ = 2048
CTX = 256
L = CTX + SEQ
GRID_W = 64
EPS = 1e-6
NEG_INF = -1e30
ROPE_THETA = 10000.0

GDN_H = 4
GDN_DK = 128
GDN_C = 64
N_CHUNK = L // GDN_C
N_CTX_CHUNK = CTX // GDN_C

SWA_QH = 8
SWA_KVH = 2
SWA_GROUP = SWA_QH // SWA_KVH
SWA_DH = 64
SWA_BLK = 128

LRU_W = 512
LRU_NBLK = 8
LRU_C = 8.0

N_EXP = 16
EXP_HID = 2048
CAP_X = 2 * SEQ // N_EXP
CAP_C = 2 * CTX // N_EXP

TM = 256
N_TILES = L // TM
MAIN_SPLITS = (2048, 768, 1024, 3072)
N_MAIN = sum(MAIN_SPLITS)

VMEM_MIB_V7X = 64


def _cparams(sem, vmem_mib):
    assert vmem_mib < VMEM_MIB_V7X
    return pltpu.CompilerParams(dimension_semantics=sem, vmem_limit_bytes=vmem_mib * 1024 * 1024)


def _dot(a, b):
    return jnp.dot(a, b, preferred_element_type=F32)


def _dot_nt(a, b):
    return lax.dot_general(a, b, (((1,), (1,)), ((), ())), preferred_element_type=F32)


def _dot_tn(a, b):
    return lax.dot_general(a, b, (((0,), (0,)), ((), ())), preferred_element_type=F32)


def _split(x):
    hi = x.astype(BF16)
    lo = (x - hi.astype(F32)).astype(BF16)
    return hi, lo


def _dot3(a, b):
    ah, al = _split(a)
    bh, bl = _split(b)
    return _dot(ah, bh) + _dot(al, bh) + _dot(ah, bl)


def _sigmoid(x):
    return 1.0 / (1.0 + jnp.exp(-x))


def _silu(x):
    return x * _sigmoid(x)


def _softplus(x):
    return jnp.maximum(x, 0.0) + jnp.log1p(jnp.exp(-jnp.abs(x)))


def _iota(shape, dim):
    return lax.broadcasted_iota(I32, shape, dim)


def _seg_conv4(x, w, row):
    n = x.shape[0]
    segpos = jnp.where(row < CTX, row, row - CTX)
    is_last = (row == CTX - 1) | (row == n - 1)
    xm2 = jnp.where(segpos >= 2, pltpu.roll(x, 2, 0), 0.0)
    xm1 = jnp.where(segpos >= 1, pltpu.roll(x, 1, 0), 0.0)
    xp1 = jnp.where(is_last, 0.0, pltpu.roll(x, n - 1, 0))
    return xm2 * w[0:1] + xm1 * w[1:2] + x * w[2:3] + xp1 * w[3:4]


def _ada_kernel(c_ref, w_ref, b_ref, o_ref):
    o_ref[...] = _dot3(_silu(c_ref[...]), w_ref[...]) + b_ref[...]


def _ada_call(cond16, w_ada, b_ada):
    depth, _, n = w_ada.shape
    tn = 1536
    return pl.pallas_call(
        _ada_kernel,
        grid=(depth, n // tn),
        in_specs=[pl.BlockSpec((16, D), lambda l, j: (0, 0)),
                  pl.BlockSpec((None, D, tn), lambda l, j: (l, 0, j)),
                  pl.BlockSpec((None, 1, tn), lambda l, j: (l, 0, j))],
        out_specs=pl.BlockSpec((None, 16, tn), lambda l, j: (l, 0, j)),
        out_shape=jax.ShapeDtypeStruct((depth, 16, n), F32),
        compiler_params=_cparams(("arbitrary", "arbitrary"), 32),
        name="ada",
    )(cond16, w_ada, b_ada.reshape(depth, 1, n))


def _modulated_norm(x, g, shift, scale):
    xn = x * lax.rsqrt(jnp.mean(x * x, axis=-1, keepdims=True) + EPS) * g
    return xn * (1.0 + scale) + shift


def _inproj_kernel(x_ref, mod_ref, g_ref, wm_ref, wab_ref, gdn_ref, swa_ref, lru_ref, gate_ref, ab_ref):
    mod = mod_ref[...]
    hb = _modulated_norm(x_ref[...], g_ref[...], mod[0:1], mod[1:2]).astype(BF16)
    col = 0
    for ref, width in zip((gdn_ref, swa_ref, lru_ref, gate_ref), MAIN_SPLITS):
        for c0 in range(0, width, 512):
            cw = min(512, width - c0)
            ref[:, c0:c0 + cw] = _dot(hb, wm_ref[:, col + c0:col + c0 + cw])
        col += width
    ab_ref[...] = _dot(hb, wab_ref[...])


def _mod_spec():
    return pl.BlockSpec((None, None, 6, D), lambda b, t: (b, jnp.minimum(t, 1), 0, 0))


def _tile_spec(width):
    return pl.BlockSpec((None, TM, width), lambda b, t: (b, t, 0))


def _const_spec(shape):
    nd = len(shape)
    return pl.BlockSpec(shape, lambda b, t: (0,) * nd, pipeline_mode=pl.Buffered(1))


def _inproj_call(xt, modsel, g1, w_main, w_ab):
    bsz = xt.shape[0]
    outs = [jax.ShapeDtypeStruct((bsz, L, w), F32) for w in MAIN_SPLITS] + [jax.ShapeDtypeStruct((bsz, L, 16), F32)]
    return pl.pallas_call(
        _inproj_kernel,
        grid=(bsz, N_TILES),
        in_specs=[_tile_spec(D), _mod_spec(), _const_spec((1, D)),
                  _const_spec((D, N_MAIN)), _const_spec((D, 16))],
        out_specs=[_tile_spec(w) for w in MAIN_SPLITS] + [_tile_spec(16)],
        out_shape=outs,
        compiler_params=_cparams(("arbitrary", "arbitrary"), 48),
        name="inproj",
    )(xt, modsel, g1, w_main, w_ab)


def _gdn_kernel(q_ref, k_ref, v_ref, z_ref, cwq_ref, cwk_ref, cwv_ref, ab_ref, abt_ref,
                alog_r_ref, dt_r_ref, alog_c_ref, dt_c_ref, og_ref, o_ref,
                qs, ks, vs, us, ws, qgs, kds, qks, egs):
    h = pl.program_id(1)
    row = _iota((L, 1), 0)

    q = _silu(_seg_conv4(q_ref[...], cwq_ref[...], row))
    qs[...] = q * lax.rsqrt(jnp.sum(q * q, axis=-1, keepdims=True) + EPS) * (GDN_DK ** -0.5)
    k = _silu(_seg_conv4(k_ref[...], cwk_ref[...], row))
    ks[...] = k * lax.rsqrt(jnp.sum(k * k, axis=-1, keepdims=True) + EPS)
    vs[...] = _silu(_seg_conv4(v_ref[...], cwv_ref[...], row))

    c = GDN_C
    ii = _iota((c, c), 0)
    jj = _iota((c, c), 1)
    eye = jnp.where(ii == jj, 1.0, 0.0)
    lane16 = _iota((c, 16), 1)
    sub16 = _iota((16, c), 0)

    for d in range(2):
        if d == 0:
            incl, strict, incl_t = jj <= ii, jj < ii, ii <= jj
        else:
            incl, strict, incl_t = jj >= ii, jj > ii, ii >= jj
        a_idx = d * GDN_H + h
        b_idx = 2 * GDN_H + a_idx

        def prep(ci, carry, incl=incl, strict=strict, incl_t=incl_t, a_idx=a_idx, b_idx=b_idx, d=d):
            rows = pl.ds(pl.multiple_of(ci * c, c), c)
            qc, kc, vc = qs[rows, :], ks[rows, :], vs[rows, :]
            ab = ab_ref[rows, :]
            g_all = -jnp.exp(alog_r_ref[...]) * _softplus(ab + dt_r_ref[...])
            gcol = jnp.sum(jnp.where(lane16 == a_idx, g_all, 0.0), axis=1, keepdims=True)
            bcol = jnp.sum(jnp.where(lane16 == b_idx, _sigmoid(ab), 0.0), axis=1, keepdims=True)
            abt = abt_ref[ci]
            g_all_t = -jnp.exp(alog_c_ref[...]) * _softplus(abt + dt_c_ref[...])
            grow = jnp.sum(jnp.where(sub16 == a_idx, g_all_t, 0.0), axis=0, keepdims=True)
            gbc = jnp.sum(jnp.where(incl, grow, 0.0), axis=1, keepdims=True)
            gbr = jnp.sum(jnp.where(incl_t, gcol, 0.0), axis=0, keepdims=True)
            decay = jnp.where(incl, jnp.exp(jnp.where(incl, gbc - gbr, 0.0)), 0.0)
            kb = kc * bcol
            kcb = kc.astype(BF16)
            a_mat = jnp.where(strict, _dot_nt(kb.astype(BF16), kcb) * decay, 0.0)
            inv = eye - a_mat
            p = a_mat
            for _ in range(int(np.log2(c)) - 1):
                pb = p.astype(BF16)
                p = _dot(pb, pb)
                inv = _dot(inv.astype(BF16), (eye + p).astype(BF16))
            invb = inv.astype(BF16)
            us[rows, :] = _dot(invb, (vc * bcol).astype(BF16))
            ws[rows, :] = _dot(invb, (kb * jnp.exp(gbc)).astype(BF16))
            qks[rows, :] = _dot_nt(qc.astype(BF16), kcb) * decay
            qgs[rows, :] = qc * jnp.exp(gbc)
            glast = gbc[c - 1:c, :] if d == 0 else gbc[0:1, :]
            kds[rows, :] = kc * jnp.exp(glast - gbc)
            egs[ci] = jnp.broadcast_to(jnp.exp(glast), (8, GDN_DK))
            return carry

        lax.fori_loop(0, N_CHUNK, prep, 0)

        def step(t, s, d=d):
            if d == 0:
                ci = t
            else:
                ci = jnp.where(t < N_CTX_CHUNK, N_CTX_CHUNK - 1 - t, N_CHUNK + N_CTX_CHUNK - 1 - t)
            rows = pl.ds(pl.multiple_of(ci * c, c), c)
            sb = s.astype(BF16)
            vnew = us[rows, :] - _dot(ws[rows, :].astype(BF16), sb)
            vnb = vnew.astype(BF16)
            o = _dot(qgs[rows, :].astype(BF16), sb) + _dot(qks[rows, :].astype(BF16), vnb)
            if d == 0:
                o_ref[rows, :] = o
            else:
                o_ref[rows, :] += o
            return s * egs[ci][0:1, :] + _dot_tn(kds[rows, :].astype(BF16), vnb)

        lax.fori_loop(0, N_CHUNK, step, jnp.zeros((GDN_DK, GDN_DK), F32))

    o = o_ref[...]
    y = o * lax.rsqrt(jnp.mean(o * o, axis=-1, keepdims=True) + EPS) * og_ref[...]
    o_ref[...] = y * _silu(z_ref[...])


def _gdn_call(gdn, ab, abt, conv_w, alog_r, dt_r, alog_c, dt_c, onorm_g):
    bsz = gdn.shape[0]

    def head_spec(off):
        return pl.BlockSpec((None, L, GDN_DK), lambda b, h: (b, 0, off + h))

    def cw_spec(off):
        return pl.BlockSpec((4, GDN_DK), lambda b, h: (0, off + h))

    def small(shape):
        nd = len(shape)
        return pl.BlockSpec(shape, lambda b, h: (0,) * nd)

    seq = lambda w: pltpu.VMEM((L, w), F32)
    return pl.pallas_call(
        _gdn_kernel,
        grid=(bsz, GDN_H),
        in_specs=[head_spec(0), head_spec(GDN_H), head_spec(2 * GDN_H), head_spec(3 * GDN_H),
                  cw_spec(0), cw_spec(GDN_H), cw_spec(2 * GDN_H),
                  pl.BlockSpec((None, L, 16), lambda b, h: (b, 0, 0)),
                  pl.BlockSpec((None, N_CHUNK, 16, GDN_C), lambda b, h: (b, 0, 0, 0)),
                  small((1, 16)), small((1, 16)), small((16, 1)), small((16, 1)), small((1, GDN_DK))],
        out_specs=pl.BlockSpec((None, L, GDN_DK), lambda b, h: (b, 0, h)),
        out_shape=jax.ShapeDtypeStruct((bsz, L, GDN_H * GDN_DK), F32),
        scratch_shapes=[seq(GDN_DK), seq(GDN_DK), seq(GDN_DK), seq(GDN_DK), seq(GDN_DK), seq(GDN_DK),
                        seq(GDN_DK), seq(GDN_C), pltpu.VMEM((N_CHUNK, 8, GDN_DK), F32)],
        compiler_params=_cparams(("arbitrary", "arbitrary"), 48),
        name="gdn",
    )(gdn, gdn, gdn, gdn, conv_w, conv_w, conv_w, ab, abt, alog_r, dt_r, alog_c, dt_c, onorm_g)


def _swa_kernel(sink_ref, in_ref, qg_ref, kg_ref, cos_ref, sin_ref, o_ref, q_s, kp_s, vp_s, kc_s, vc_s):
    ii = _iota((128, 128), 0)
    jj = _iota((128, 128), 1)
    head_ones = jnp.where((ii // SWA_DH) == (jj // SWA_DH), 1.0, 0.0).astype(BF16)
    even = (_iota((1, 128), 1) % 2) == 0

    def norm(t, g):
        ssq = _dot((t * t).astype(BF16), head_ones)
        return t * lax.rsqrt(ssq * (1.0 / SWA_DH) + EPS) * g

    def rope(t):
        rot = jnp.where(even, -pltpu.roll(t, 127, 1), pltpu.roll(t, 1, 1))
        return t * cos_ref[...] + rot * sin_ref[...]

    scale = SWA_DH ** -0.5
    for cg in range(SWA_QH * SWA_DH // 128):
        cols = slice(cg * 128, (cg + 1) * 128)
        q_s[0:CTX, cols] = (norm(in_ref[0:CTX, cols], qg_ref[...]) * scale).astype(BF16)
        q_s[CTX:L, cols] = (rope(norm(in_ref[CTX:L, cols], qg_ref[...])) * scale).astype(BF16)
    kcols = slice(512, 640)
    vcols = slice(640, 768)
    zeros_blk = jnp.zeros((SWA_BLK, 128), BF16)
    kc_s[...] = norm(in_ref[0:CTX, kcols], kg_ref[...]).astype(BF16)
    kp_s[0:SWA_BLK, :] = zeros_blk
    kp_s[SWA_BLK:SWA_BLK + SEQ, :] = rope(norm(in_ref[CTX:L, kcols], kg_ref[...])).astype(BF16)
    kp_s[SWA_BLK + SEQ:, :] = zeros_blk
    vc_s[...] = in_ref[0:CTX, vcols].astype(BF16)
    vp_s[0:SWA_BLK, :] = zeros_blk
    vp_s[SWA_BLK:SWA_BLK + SEQ, :] = in_ref[CTX:L, vcols].astype(BF16)
    vp_s[SWA_BLK + SEQ:, :] = zeros_blk

    a_idx = _iota((SWA_BLK, 3 * SWA_BLK), 0)
    s_idx = _iota((SWA_BLK, 3 * SWA_BLK), 1)
    in_window = jnp.abs(s_idx - SWA_BLK - a_idx) <= SWA_BLK

    def latent_block(n, carry):
        q0 = pl.multiple_of(CTX + n * SWA_BLK, SWA_BLK)
        k0 = pl.multiple_of(n * SWA_BLK, SWA_BLK)
        qblk = q_s[pl.ds(q0, SWA_BLK), :]
        kw = kp_s[pl.ds(k0, 3 * SWA_BLK), :]
        vw = vp_s[pl.ds(k0, 3 * SWA_BLK), :]
        kpos = n * SWA_BLK - SWA_BLK + s_idx
        mask = in_window & (kpos >= 0) & (kpos < SEQ)
        for hq in range(SWA_QH):
            kv = hq // SWA_GROUP
            hs = slice(hq * SWA_DH, (hq + 1) * SWA_DH)
            kvs = slice(kv * SWA_DH, (kv + 1) * SWA_DH)
            sink = sink_ref[hq]
            qh = qblk[:, hs]
            s_loc = jnp.where(mask, _dot_nt(qh, kw[:, kvs]), NEG_INF)
            s_ctx = _dot_nt(qh, kc_s[:, kvs])
            m = jnp.maximum(jnp.maximum(jnp.max(s_loc, axis=-1, keepdims=True),
                                        jnp.max(s_ctx, axis=-1, keepdims=True)), sink)
            p_loc = jnp.exp(s_loc - m)
            p_ctx = jnp.exp(s_ctx - m)
            den = (jnp.sum(p_loc, axis=-1, keepdims=True) + jnp.sum(p_ctx, axis=-1, keepdims=True)
                   + jnp.exp(sink - m))
            acc = _dot(p_loc.astype(BF16), vw[:, kvs]) + _dot(p_ctx.astype(BF16), vc_s[:, kvs])
            o_ref[pl.ds(q0, SWA_BLK), hs] = acc / den
        return carry

    lax.fori_loop(0, SEQ // SWA_BLK, latent_block, 0)

    for n in range(CTX // SWA_BLK):
        rows = slice(n * SWA_BLK, (n + 1) * SWA_BLK)
        qblk = q_s[rows, :]
        for hq in range(SWA_QH):
            kv = hq // SWA_GROUP
            hs = slice(hq * SWA_DH, (hq + 1) * SWA_DH)
            kvs = slice(kv * SWA_DH, (kv + 1) * SWA_DH)
            sink = sink_ref[hq]
            s = _dot_nt(qblk[:, hs], kc_s[:, kvs])
            m = jnp.maximum(jnp.max(s, axis=-1, keepdims=True), sink)
            p = jnp.exp(s - m)
            den = jnp.sum(p, axis=-1, keepdims=True) + jnp.exp(sink - m)
            o_ref[rows, hs] = _dot(p.astype(BF16), vc_s[:, kvs]) / den


def _swa_call(swa, qg, kg, sink, cos_t, sin_t):
    bsz = swa.shape[0]
    wq = SWA_QH * SWA_DH

    def small(shape):
        nd = len(shape)
        return pl.BlockSpec(shape, lambda b: (0,) * nd)

    return pl.pallas_call(
        _swa_kernel,
        grid=(bsz,),
        in_specs=[pl.BlockSpec(memory_space=pltpu.SMEM),
                  pl.BlockSpec((None, L, MAIN_SPLITS[1]), lambda b: (b, 0, 0)),
                  small((1, 128)), small((1, 128)), small((SEQ, 128)), small((SEQ, 128))],
        out_specs=pl.BlockSpec((None, L, wq), lambda b: (b, 0, 0)),
        out_shape=jax.ShapeDtypeStruct((bsz, L, wq), F32),
        scratch_shapes=[pltpu.VMEM((L, wq), BF16),
                        pltpu.VMEM((SEQ + 2 * SWA_BLK, 128), BF16), pltpu.VMEM((SEQ + 2 * SWA_BLK, 128), BF16),
                        pltpu.VMEM((CTX, 128), BF16), pltpu.VMEM((CTX, 128), BF16)],
        compiler_params=_cparams(("arbitrary",), 48),
        name="swa",
    )(sink, swa, qg, kg, cos_t, sin_t)


LRU_RB = 32


def _lru_kernel(in_ref, cw_ref, cb_ref, wg_ref, gb_ref, lam_ref, o_ref, u_s, a_s, b_s):
    row = _iota((L, 1), 0)
    u_s[...] = _seg_conv4(in_ref[:, 0:LRU_W], cw_ref[...], row) + cb_ref[...]

    sub = _iota((LRU_RB, 1), 0) % 8
    n_blk = L // LRU_RB
    n_ctx_blk = CTX // LRU_RB

    for d in range(2):
        sp = _softplus(-lam_ref[d])

        def coeffs(t, carry, d=d, sp=sp):
            rows = pl.ds(pl.multiple_of(t * TM, TM), TM)
            ub = u_s[rows, :]
            pre = _dot(ub.astype(BF16), wg_ref[d]) + gb_ref[d]
            r = _sigmoid(pre[:, 0:LRU_W])
            i = _sigmoid(pre[:, LRU_W:])
            log_a = -LRU_C * r * sp
            a = jnp.exp(log_a)
            a_s[rows, :] = a
            b_s[rows, :] = jnp.sqrt(jnp.tanh(-log_a) * (a * a + 1.0)) * (i * ub)
            return carry

        lax.fori_loop(0, N_TILES, coeffs, 0)

        def scan_block(t, hin, d=d):
            if d == 0:
                blk = t
            else:
                blk = jnp.where(t < n_ctx_blk, n_ctx_blk - 1 - t, n_blk + n_ctx_blk - 1 - t)
            rows = pl.ds(pl.multiple_of(blk * LRU_RB, LRU_RB), LRU_RB)
            a = a_s[rows, :]
            b = b_s[rows, :]
            for s in (1, 2, 4):
                if d == 0:
                    valid = sub >= s
                    shift = s
                else:
                    valid = sub < 8 - s
                    shift = LRU_RB - s
                a_sh = jnp.where(valid, pltpu.roll(a, shift, 0), 1.0)
                b_sh = jnp.where(valid, pltpu.roll(b, shift, 0), 0.0)
                b = a * b_sh + b
                a = a * a_sh
            tiles = range(LRU_RB // 8) if d == 0 else range(LRU_RB // 8 - 1, -1, -1)
            hs = [None] * (LRU_RB // 8)
            for ti in tiles:
                ht = a[ti * 8:(ti + 1) * 8, :] * hin + b[ti * 8:(ti + 1) * 8, :]
                hs[ti] = ht
                hin = ht[7:8, :] if d == 0 else ht[0:1, :]
            hblk = jnp.concatenate(hs, axis=0)
            if d == 0:
                o_ref[rows, :] = hblk
            else:
                o_ref[rows, :] += hblk
            return hin

        lax.fori_loop(0, n_blk, scan_block, jnp.zeros((1, LRU_W), F32))

    y = in_ref[:, LRU_W:]
    gelu = 0.5 * y * (1.0 + jnp.tanh(np.sqrt(2.0 / np.pi) * (y + 0.044715 * (y * y * y))))
    o_ref[...] = o_ref[...] * gelu


def _lru_call(lru, conv_w, conv_b, wg, gb, lam):
    bsz = lru.shape[0]

    def small(shape):
        nd = len(shape)
        return pl.BlockSpec(shape, lambda b: (0,) * nd)

    return pl.pallas_call(
        _lru_kernel,
        grid=(bsz,),
        in_specs=[pl.BlockSpec((None, L, 2 * LRU_W), lambda b: (b, 0, 0)),
                  small((4, LRU_W)), small((1, LRU_W)), small((2, LRU_W, 2 * LRU_W)),
                  small((2, 1, 2 * LRU_W)), small((2, 1, LRU_W))],
        out_specs=pl.BlockSpec((None, L, LRU_W), lambda b: (b, 0, 0)),
        out_shape=jax.ShapeDtypeStruct((bsz, L, LRU_W), F32),
        scratch_shapes=[pltpu.VMEM((L, LRU_W), F32)] * 3,
        compiler_params=_cparams(("arbitrary",), 56),
        name="lru",
    )(lru, conv_w, conv_b, wg, gb, lam)


def _merge_kernel(a_ref, b_ref, c_ref, gl_ref, x_ref, mod_ref, g2_ref, wb_ref, wo_ref, wr_ref,
                  xn_ref, h2_ref, aff_ref):
    mixed = None
    for i, br in enumerate((a_ref, b_ref, c_ref)):
        term = _sigmoid(gl_ref[:, i * D:(i + 1) * D]) * _dot(br[...].astype(BF16), wb_ref[i])
        mixed = term if mixed is None else mixed + term
    mod = mod_ref[...]
    xn = x_ref[...] + mod[2:3] * _dot(mixed.astype(BF16), wo_ref[...])
    xn_ref[...] = xn
    h2 = _modulated_norm(xn, g2_ref[...], mod[3:4], mod[4:5])
    h2_ref[...] = h2.astype(BF16)
    logits = _dot3(h2, wr_ref[...])
    e = jnp.exp(logits - jnp.max(logits, axis=-1, keepdims=True))
    aff_ref[...] = e / jnp.sum(e, axis=-1, keepdims=True)


def _merge_call(a_br, b_br, c_br, gate, xt, modsel, g2, wb, wo, wr):
    bsz = xt.shape[0]
    return pl.pallas_call(
        _merge_kernel,
        grid=(bsz, N_TILES),
        in_specs=[_tile_spec(512), _tile_spec(512), _tile_spec(512), _tile_spec(3 * D), _tile_spec(D),
                  _mod_spec(), _const_spec((1, D)), _const_spec((3, 512, D)), _const_spec((D, D)),
                  _const_spec((D, N_EXP))],
        out_specs=[_tile_spec(D), _tile_spec(D), _tile_spec(N_EXP)],
        out_shape=[jax.ShapeDtypeStruct((bsz, L, D), F32), jax.ShapeDtypeStruct((bsz, L, D), BF16),
                   jax.ShapeDtypeStruct((bsz, L, N_EXP), F32)],
        compiler_params=_cparams(("arbitrary", "arbitrary"), 40),
        name="merge",
    )(a_br, b_br, c_br, gate, xt, modsel, g2, wb, wo, wr)


def _excl_prefix(m):
    w = 256
    pp = _iota((w, w), 0)
    jj = _iota((w, w), 1)
    upper = jnp.where(pp < jj, 1.0, 0.0).astype(BF16)
    off = jnp.zeros((m.shape[0], 1), F32)
    parts = []
    for c0 in range(0, m.shape[1], w):
        blk = m[:, c0:c0 + w]
        parts.append(_dot(blk.astype(BF16), upper) + off)
        off = off + jnp.sum(blk, axis=1, keepdims=True)
    return parts[0] if len(parts) == 1 else jnp.concatenate(parts, axis=1)


def _topk_pos(v, k):
    bits = pltpu.bitcast(v, I32)
    thr = jnp.zeros((v.shape[0], 1), I32)
    for bit in range(30, -1, -1):
        cand = thr | (1 << bit)
        cnt = jnp.sum(jnp.where(bits >= cand, 1.0, 0.0), axis=1, keepdims=True)
        thr = jnp.where(cnt >= k, cand, thr)
    gt = bits > thr
    eq = jnp.where(bits == thr, 1.0, 0.0)
    need = k - jnp.sum(jnp.where(gt, 1.0, 0.0), axis=1, keepdims=True)
    sel = jnp.where(gt, 1.0, jnp.where(_excl_prefix(eq) < need, eq, 0.0))
    return jnp.where(sel > 0.0, _excl_prefix(sel), -1.0).astype(I32)


def _select_kernel(aff_ref, pos_ref):
    pos_ref[:, 0:CTX] = _topk_pos(aff_ref[:, 0:CTX], CAP_C)
    pos_ref[:, CTX:L] = _topk_pos(aff_ref[:, CTX:L], CAP_X)


def _select_call(aff_t):
    bsz = aff_t.shape[0]
    return pl.pallas_call(
        _select_kernel,
        grid=(bsz,),
        in_specs=[pl.BlockSpec((None, N_EXP, L), lambda b: (b, 0, 0))],
        out_specs=pl.BlockSpec((None, N_EXP, L), lambda b: (b, 0, 0)),
        out_shape=jax.ShapeDtypeStruct((bsz, N_EXP, L), I32),
        compiler_params=_cparams(("arbitrary",), 32),
        name="select",
    )(aff_t)


def _gather_kernel(pos_ref, h_ref, xs_ref, xc_ref):
    pos = pos_ref[...]
    p_x = jnp.where(_iota((CAP_X, SEQ), 0) == pos[:, CTX:L], 1.0, 0.0).astype(BF16)
    xs_ref[...] = _dot(p_x, h_ref[CTX:L, :]).astype(BF16)
    p_c = jnp.where(_iota((CAP_C, CTX), 0) == pos[:, 0:CTX], 1.0, 0.0).astype(BF16)
    xc_ref[...] = _dot(p_c, h_ref[0:CTX, :]).astype(BF16)


def _gather_call(pos4, h2):
    bsz = h2.shape[0]
    return pl.pallas_call(
        _gather_kernel,
        grid=(bsz, N_EXP),
        in_specs=[pl.BlockSpec((None, None, 1, L), lambda b, e: (b, e, 0, 0)),
                  pl.BlockSpec((None, L, D), lambda b, e: (b, 0, 0))],
        out_specs=[pl.BlockSpec((None, CAP_X, D), lambda b, e: (e, b, 0)),
                   pl.BlockSpec((None, CAP_C, D), lambda b, e: (e, b, 0))],
        out_shape=[jax.ShapeDtypeStruct((N_EXP, bsz * CAP_X, D), BF16),
                   jax.ShapeDtypeStruct((N_EXP, bsz * CAP_C, D), BF16)],
        compiler_params=_cparams(("arbitrary", "arbitrary"), 40),
        name="gather",
    )(pos4, h2)


FFN_HC = 512
FFN_MT = 512


def _ffn_kernel(xs_ref, xc_ref, wg_ref, wu_ref, wd_ref, ys_ref, yc_ref, acc_x, acc_c, wgb, wub, wdb):
    hc = pl.program_id(1)
    wgb[...] = wg_ref[...].astype(BF16)
    wub[...] = wu_ref[...].astype(BF16)
    wdb[...] = wd_ref[...].astype(BF16)

    def ffn(x):
        g = _dot(x, wgb[...])
        hid = _silu(g) * _dot(x, wub[...])
        return _dot(hid.astype(BF16), wdb[...])

    @pl.when(hc == 0)
    def _():
        acc_x[...] = jnp.zeros_like(acc_x)
        acc_c[...] = jnp.zeros_like(acc_c)

    for m0 in range(0, xs_ref.shape[0], FFN_MT):
        acc_x[m0:m0 + FFN_MT, :] += ffn(xs_ref[m0:m0 + FFN_MT, :])
    acc_c[...] += ffn(xc_ref[...])

    @pl.when(hc == pl.num_programs(1) - 1)
    def _():
        ys_ref[...] = acc_x[...].astype(BF16)
        yc_ref[...] = acc_c[...].astype(BF16)


def _ffn_call(xs, xc, w_gate, w_up, w_down, layer):
    mx, mc = xs.shape[1], xc.shape[1]
    return pl.pallas_call(
        _ffn_kernel,
        grid=(N_EXP, EXP_HID // FFN_HC),
        in_specs=[pl.BlockSpec((None, mx, D), lambda e, j: (e, 0, 0)),
                  pl.BlockSpec((None, mc, D), lambda e, j: (e, 0, 0)),
                  pl.BlockSpec((None, None, D, FFN_HC), lambda e, j: (layer, e, 0, j)),
                  pl.BlockSpec((None, None, D, FFN_HC), lambda e, j: (layer, e, 0, j)),
                  pl.BlockSpec((None, None, FFN_HC, D), lambda e, j: (layer, e, j, 0))],
        out_specs=[pl.BlockSpec((None, mx, D), lambda e, j: (e, 0, 0)),
                   pl.BlockSpec((None, mc, D), lambda e, j: (e, 0, 0))],
        out_shape=[jax.ShapeDtypeStruct(xs.shape, BF16), jax.ShapeDtypeStruct(xc.shape, BF16)],
        scratch_shapes=[pltpu.VMEM((mx, D), F32), pltpu.VMEM((mc, D), F32),
                        pltpu.VMEM((D, FFN_HC), BF16), pltpu.VMEM((D, FFN_HC), BF16),
                        pltpu.VMEM((FFN_HC, D), BF16)],
        compiler_params=_cparams(("arbitrary", "arbitrary"), 56),
        name="ffn",
    )(xs, xc, w_gate, w_up, w_down)


def _scatter_kernel(ys_ref, yc_ref, pos_ref, aff_ref, x_ref, mod_ref, o_ref):
    e = pl.program_id(1)
    lane = _iota((TM, N_EXP), 1)

    @pl.when(e == 0)
    def _():
        o_ref[...] = jnp.zeros_like(o_ref)

    for t in range(N_TILES):
        rows = slice(t * TM, (t + 1) * TM)
        pcol = jnp.sum(jnp.where(lane == e, pos_ref[rows, :].astype(F32), 0.0), axis=1, keepdims=True)
        gcol = jnp.sum(jnp.where(lane == e, aff_ref[rows, :], 0.0), axis=1, keepdims=True)
        if t == 0:
            p_t = jnp.where(pcol == _iota((TM, CAP_C), 1).astype(F32), 1.0, 0.0).astype(BF16)
            contrib = _dot(p_t, yc_ref[...])
        else:
            p_t = jnp.where(pcol == _iota((TM, CAP_X), 1).astype(F32), 1.0, 0.0).astype(BF16)
            contrib = _dot(p_t, ys_ref[...])
        o_ref[rows, :] += gcol * contrib

    @pl.when(e == pl.num_programs(1) - 1)
    def _():
        o_ref[0:CTX, :] = x_ref[0:CTX, :] + mod_ref[0:1, :] * o_ref[0:CTX, :]
        o_ref[CTX:L, :] = x_ref[CTX:L, :] + mod_ref[1:2, :] * o_ref[CTX:L, :]


def _scatter_call(ys, yc, pos_t, aff, xn, mod_x):
    bsz = xn.shape[0]
    return pl.pallas_call(
        _scatter_kernel,
        grid=(bsz, N_EXP),
        in_specs=[pl.BlockSpec((None, CAP_X, D), lambda b, e: (e, b, 0)),
                  pl.BlockSpec((None, CAP_C, D), lambda b, e: (e, b, 0)),
                  pl.BlockSpec((None, L, N_EXP), lambda b, e: (b, 0, 0)),
                  pl.BlockSpec((None, L, N_EXP), lambda b, e: (b, 0, 0)),
                  pl.BlockSpec((None, L, D), lambda b, e: (b, 0, 0)),
                  pl.BlockSpec((None, 2, D), lambda b, e: (b, 0, 0))],
        out_specs=pl.BlockSpec((None, L, D), lambda b, e: (b, 0, 0)),
        out_shape=jax.ShapeDtypeStruct((bsz, L, D), F32),
        compiler_params=_cparams(("arbitrary", "arbitrary"), 56),
        name="scatter",
    )(ys, yc, pos_t, aff, xn, mod_x)


def _rope_tables():
    rows = SEQ // GRID_W
    row = jnp.repeat(jnp.arange(rows, dtype=F32), GRID_W)
    col = jnp.tile(jnp.arange(GRID_W, dtype=F32), rows)
    axis_dim = SWA_DH // 2
    inv_freq = ROPE_THETA ** (-jnp.arange(0, axis_dim, 2, dtype=F32) / axis_dim)
    ang = jnp.concatenate([row[:, None] * inv_freq, col[:, None] * inv_freq], axis=-1)
    ang = jnp.tile(jnp.repeat(ang, 2, axis=-1), (1, 128 // SWA_DH))
    return jnp.cos(ang), jnp.sin(ang)


def _block_diag(w):
    n, bw, _ = w.shape
    eye = jnp.eye(n, dtype=w.dtype)
    return (eye[:, None, :, None] * w[:, :, None, :]).reshape(n * bw, n * bw)


def _pad16(v):
    return jnp.concatenate([v.reshape(-1), jnp.zeros((16 - v.size,), v.dtype)])


def _layer(xt, mods_l, layer, norm1_g, norm2_g, w_in, gdn_conv_w, gdn_a_log, gdn_dt_bias, gdn_onorm_g,
           swa_qnorm_g, swa_knorm_g, swa_sink, lru_conv_w, lru_conv_b, lru_gate_w, lru_gate_b, lru_lambda,
           w_branch, w_out, w_router, w_exp_gate, w_exp_up, w_exp_down, cos_t, sin_t):
    bsz = xt.shape[0]
    mods6 = mods_l.reshape(16, 6, D)
    modsel = jnp.stack([jnp.broadcast_to(mods6[8], (bsz, 6, D)), mods6[:bsz]], axis=1)

    w_main = jnp.concatenate([w_in[:, :2048], w_in[:, 2064:]], axis=1).astype(BF16)
    w_ab = w_in[:, 2048:2064].astype(BF16)
    gdn, swa, lru, gate, ab = _inproj_call(xt, modsel, norm1_g.reshape(1, D), w_main, w_ab)

    abt = jnp.swapaxes(ab.reshape(bsz, N_CHUNK, GDN_C, 16), 2, 3)
    alog16, dt16 = _pad16(gdn_a_log), _pad16(gdn_dt_bias)
    a_br = _gdn_call(gdn, ab, abt, gdn_conv_w, alog16.reshape(1, 16), dt16.reshape(1, 16),
                     alog16.reshape(16, 1), dt16.reshape(16, 1), gdn_onorm_g.reshape(1, GDN_DK))

    b_br = _swa_call(swa, jnp.tile(swa_qnorm_g, 128 // SWA_DH).reshape(1, 128),
                     jnp.tile(swa_knorm_g, 128 // SWA_DH).reshape(1, 128), swa_sink, cos_t, sin_t)

    wg = jnp.stack([jnp.concatenate([_block_diag(lru_gate_w[d, 0]), _block_diag(lru_gate_w[d, 1])], axis=1)
                    for d in range(2)]).astype(BF16)
    c_br = _lru_call(lru, lru_conv_w, lru_conv_b.reshape(1, LRU_W), wg,
                     lru_gate_b.reshape(2, 1, 2 * LRU_W), lru_lambda.reshape(2, 1, LRU_W))

    xn, h2, aff = _merge_call(a_br, b_br, c_br, gate, xt, modsel, norm2_g.reshape(1, D),
                              w_branch.astype(BF16), w_out.astype(BF16), w_router)

    pos = _select_call(jnp.swapaxes(aff, 1, 2))
    xs, xc = _gather_call(pos.reshape(bsz, N_EXP, 1, L), h2)
    ys, yc = _ffn_call(xs, xc, w_exp_gate, w_exp_up, w_exp_down, layer)
    gate2 = modsel[:, :, 5, :]
    return _scatter_call(ys, yc, jnp.swapaxes(pos, 1, 2), aff, xn, gate2)


def kernel(x, c, ctx, c_ctx, w_ada, b_ada, norm1_g, norm2_g, w_in, gdn_conv_w, gdn_a_log, gdn_dt_bias, gdn_onorm_g, swa_qnorm_g, swa_knorm_g, swa_sink, lru_conv_w, lru_conv_b, lru_gate_w, lru_gate_b, lru_lambda, w_branch, w_out, w_router, w_exp_gate, w_exp_up, w_exp_down):
    bsz = x.shape[0]
    depth = w_ada.shape[0]
    assert x.shape == (bsz, SEQ, D) and ctx.shape == (bsz, CTX, D) and bsz <= 8
    cond16 = jnp.concatenate([c, jnp.zeros((8 - bsz, D), F32), c_ctx[None, :], jnp.zeros((7, D), F32)], axis=0)
    mods = _ada_call(cond16, w_ada, b_ada)
    cos_t, sin_t = _rope_tables()
    xt = jnp.concatenate([ctx, x], axis=1)
    for l in range(depth):
        xt = _layer(xt, mods[l], l, norm1_g[l], norm2_g[l], w_in[l], gdn_conv_w[l], gdn_a_log[l],
                    gdn_dt_bias[l], gdn_onorm_g[l], swa_qnorm_g[l], swa_knorm_g[l], swa_sink[l],
                    lru_conv_w[l], lru_conv_b[l], lru_gate_w[l], lru_gate_b[l], lru_lambda[l],
                    w_branch[l], w_out[l], w_router[l], w_exp_gate, w_exp_up, w_exp_down, cos_t, sin_t)
    return xt[:, CTX:, :]
```

```python
import functools

import jax
import jax.numpy as jnp
import numpy as np
from jax import lax
from jax.experimental import pallas as pl
from jax.experimental.pallas import tpu as pltpu

F32 = jnp.float32
BF16 = jnp.bfloat16
I32 = jnp.int32

D = 1024
SEQ = 2048
CTX = 256
L = CTX + SEQ
GRID_W = 64
EPS = 1e-6
NEG_INF = -1e30
ROPE_THETA = 10000.0

GDN_H = 4
GDN_DK = 128
GDN_C = 64
N_CHUNK = L // GDN_C
N_CTX_CHUNK = CTX // GDN_C

SWA_QH = 8
SWA_KVH = 2
SWA_GROUP = SWA_QH // SWA_KVH
SWA_DH = 64
SWA_BLK = 128

LRU_W = 512
LRU_NBLK = 8
LRU_C = 8.0

N_EXP = 16
EXP_HID = 2048
CAP_X = 2 * SEQ // N_EXP
CAP_C = 2 * CTX // N_EXP

TM = 256
N_TILES = L // TM
MAIN_SPLITS = (2048, 768, 1024, 3072)
N_MAIN = sum(MAIN_SPLITS)

VMEM_MIB_V7X = 64


def _cparams(sem, vmem_mib):
    assert vmem_mib < VMEM_MIB_V7X
    return pltpu.CompilerParams(dimension_semantics=sem, vmem_limit_bytes=vmem_mib * 1024 * 1024)


def _dot(a, b):
    return jnp.dot(a, b, preferred_element_type=F32)


def _dot_nt(a, b):
    return lax.dot_general(a, b, (((1,), (1,)), ((), ())), preferred_element_type=F32)


def _dot_tn(a, b):
    return lax.dot_general(a, b, (((0,), (0,)), ((), ())), preferred_element_type=F32)


def _split(x):
    hi = x.astype(BF16)
    lo = (x - hi.astype(F32)).astype(BF16)
    return hi, lo


def _dot3(a, b):
    ah, al = _split(a)
    bh, bl = _split(b)
    return _dot(ah, bh) + _dot(al, bh) + _dot(ah, bl)


def _sigmoid(x):
    return 1.0 / (1.0 + jnp.exp(-x))


def _silu(x):
    return x * _sigmoid(x)


def _softplus(x):
    return jnp.maximum(x, 0.0) + jnp.log1p(jnp.exp(-jnp.abs(x)))


def _iota(shape, dim):
    return lax.broadcasted_iota(I32, shape, dim)


def _seg_conv4(x, w, row):
    n = x.shape[0]
    segpos = jnp.where(row < CTX, row, row - CTX)
    is_last = (row == CTX - 1) | (row == n - 1)
    xm2 = jnp.where(segpos >= 2, pltpu.roll(x, 2, 0), 0.0)
    xm1 = jnp.where(segpos >= 1, pltpu.roll(x, 1, 0), 0.0)
    xp1 = jnp.where(is_last, 0.0, pltpu.roll(x, n - 1, 0))
    return xm2 * w[0:1] + xm1 * w[1:2] + x * w[2:3] + xp1 * w[3:4]


def _ada_kernel(c_ref, w_ref, b_ref, o_ref):
    o_ref[...] = _dot3(_silu(c_ref[...]), w_ref[...]) + b_ref[...]


def _ada_call(cond16, w_ada, b_ada):
    depth, _, n = w_ada.shape
    tn = 1536
    return pl.pallas_call(
        _ada_kernel,
        grid=(depth, n // tn),
        in_specs=[pl.BlockSpec((16, D), lambda l, j: (0, 0)),
                  pl.BlockSpec((None, D, tn), lambda l, j: (l, 0, j)),
                  pl.BlockSpec((None, 1, tn), lambda l, j: (l, 0, j))],
        out_specs=pl.BlockSpec((None, 16, tn), lambda l, j: (l, 0, j)),
        out_shape=jax.ShapeDtypeStruct((depth, 16, n), F32),
        compiler_params=_cparams(("arbitrary", "arbitrary"), 32),
        name="ada",
    )(cond16, w_ada, b_ada.reshape(depth, 1, n))


def _modulated_norm(x, g, shift, scale):
    xn = x * lax.rsqrt(jnp.mean(x * x, axis=-1, keepdims=True) + EPS) * g
    return xn * (1.0 + scale) + shift


def _inproj_kernel(x_ref, mod_ref, g_ref, wm_ref, wab_ref, gdn_ref, swa_ref, lru_ref, gate_ref, ab_ref):
    mod = mod_ref[...]
    hb = _modulated_norm(x_ref[...], g_ref[...], mod[0:1], mod[1:2]).astype(BF16)
    col = 0
    for ref, width in zip((gdn_ref, swa_ref, lru_ref, gate_ref), MAIN_SPLITS):
        for c0 in range(0, width, 512):
            cw = min(512, width - c0)
            ref[:, c0:c0 + cw] = _dot(hb, wm_ref[:, col + c0:col + c0 + cw])
        col += width
    ab_ref[...] = _dot(hb, wab_ref[...])


def _mod_spec():
    return pl.BlockSpec((None, None, 6, D), lambda b, t: (b, jnp.minimum(t, 1), 0, 0))


def _tile_spec(width):
    return pl.BlockSpec((None, TM, width), lambda b, t: (b, t, 0))


def _const_spec(shape):
    nd = len(shape)
    return pl.BlockSpec(shape, lambda b, t: (0,) * nd, pipeline_mode=pl.Buffered(1))


def _inproj_call(xt, modsel, g1, w_main, w_ab):
    bsz = xt.shape[0]
    outs = [jax.ShapeDtypeStruct((bsz, L, w), F32) for w in MAIN_SPLITS] + [jax.ShapeDtypeStruct((bsz, L, 16), F32)]
    return pl.pallas_call(
        _inproj_kernel,
        grid=(bsz, N_TILES),
        in_specs=[_tile_spec(D), _mod_spec(), _const_spec((1, D)),
                  _const_spec((D, N_MAIN)), _const_spec((D, 16))],
        out_specs=[_tile_spec(w) for w in MAIN_SPLITS] + [_tile_spec(16)],
        out_shape=outs,
        compiler_params=_cparams(("arbitrary", "arbitrary"), 48),
        name="inproj",
    )(xt, modsel, g1, w_main, w_ab)


GDN_HPS = 2
GDN_PREP_CHUNKS = 3


def _gdn_kernel(q_ref, k_ref, v_ref, z_ref, cwq_ref, cwk_ref, cwv_ref, ab_ref, abt_ref,
                alog_r_ref, dt_r_ref, alog_c_ref, dt_c_ref, og_ref, o_ref,
                qs, ks, vs, kq_s, n_s, eg_s, orev_s):
    head0 = pl.program_id(1) * GDN_HPS
    row = _iota((L, 1), 0)

    for hh in range(GDN_HPS):
        hc = slice(hh * GDN_DK, (hh + 1) * GDN_DK)
        q = _silu(_seg_conv4(q_ref[:, hc], cwq_ref[:, hc], row))
        qs[hh] = q * lax.rsqrt(jnp.sum(q * q, axis=-1, keepdims=True) + EPS) * (GDN_DK ** -0.5)
        k = _silu(_seg_conv4(k_ref[:, hc], cwk_ref[:, hc], row))
        ks[hh] = k * lax.rsqrt(jnp.sum(k * k, axis=-1, keepdims=True) + EPS)
        vs[hh] = _silu(_seg_conv4(v_ref[:, hc], cwv_ref[:, hc], row))

    c = GDN_C
    ii = _iota((c, c), 0)
    jj = _iota((c, c), 1)
    eye = jnp.where(ii == jj, 1.0, 0.0)
    blk = [(ii // b) == (jj // b) for b in (8, 16, 32, 64)]
    lane16 = _iota((c, 16), 1)
    sub16 = _iota((16, c), 0)

    def obuf(hh, d):
        return o_ref.at[:, hh * GDN_DK:(hh + 1) * GDN_DK] if d == 0 else orev_s.at[hh]

    def prep(it, carry):
        chunks = []
        for u in range(GDN_PREP_CHUNKS):
            ci = it * GDN_PREP_CHUNKS + u
            rows = pl.ds(pl.multiple_of(ci * c, c), c)
            ab = ab_ref[rows, :]
            g_all = -jnp.exp(alog_r_ref[...]) * _softplus(ab + dt_r_ref[...])
            beta_all = _sigmoid(ab)
            g_all_t = -jnp.exp(alog_c_ref[...]) * _softplus(abt_ref[ci] + dt_c_ref[...])
            for hh in range(GDN_HPS):
                qc, kc, vc = qs[hh, rows, :], ks[hh, rows, :], vs[hh, rows, :]
                chunks.append(dict(ci=ci, rows=rows, hh=hh, qc=qc, kc=kc, vc=vc, g_all=g_all, beta_all=beta_all,
                                   g_all_t=g_all_t, kcb=kc.astype(BF16), qcb=qc.astype(BF16)))
        for ch in chunks:
            ch["kk"] = _dot_nt(ch["kcb"], ch["kcb"])
        for ch in chunks:
            ch["qk"] = _dot_nt(ch["qcb"], ch["kcb"])
        chains = []
        for ch in chunks:
            g_all, beta_all, g_all_t = ch["g_all"], ch["beta_all"], ch["g_all_t"]
            for d in range(2):
                if d == 0:
                    incl, strict, incl_t = jj <= ii, jj < ii, ii <= jj
                else:
                    incl, strict, incl_t = jj >= ii, jj > ii, ii >= jj
                a_idx = d * GDN_H + head0 + ch["hh"]
                gcol = jnp.sum(jnp.where(lane16 == a_idx, g_all, 0.0), axis=1, keepdims=True)
                bcol = jnp.sum(jnp.where(lane16 == 2 * GDN_H + a_idx, beta_all, 0.0), axis=1, keepdims=True)
                grow = jnp.sum(jnp.where(sub16 == a_idx, g_all_t, 0.0), axis=0, keepdims=True)
                gbc = jnp.sum(jnp.where(incl, grow, 0.0), axis=1, keepdims=True)
                gbr = jnp.sum(jnp.where(incl_t, gcol, 0.0), axis=0, keepdims=True)
                decay = jnp.where(incl, jnp.exp(jnp.where(incl, gbc - gbr, 0.0)), 0.0)
                a_mat = jnp.where(strict, ch["kk"] * bcol * decay, 0.0)
                egbc = jnp.exp(gbc)
                glast = gbc[c - 1:c, :] if d == 0 else gbc[0:1, :]
                chains.append(dict(
                    ch=ch, d=d, a=a_mat, egbc=egbc, eglast=jnp.exp(glast),
                    rhs=jnp.concatenate([ch["vc"] * bcol, ch["kc"] * (bcol * egbc)], axis=1).astype(BF16),
                    kd=(ch["kc"] * jnp.exp(glast - gbc)).astype(BF16),
                    qkd=(ch["qk"] * decay).astype(BF16)))
        for cn in chains:
            a8 = jnp.where(blk[0], cn["a"], 0.0)
            a8b = a8.astype(BF16)
            cn["p2"] = _dot(a8b, a8b)
            cn["ima"] = (eye - a8).astype(BF16)
        for cn in chains:
            p2b = cn["p2"].astype(BF16)
            cn["p4"] = _dot(p2b, p2b)
            cn["inv"] = _dot(cn["ima"], (eye + cn["p2"]).astype(BF16))
        for cn in chains:
            cn["inv"] = _dot(cn["inv"].astype(BF16), (eye + cn["p4"]).astype(BF16))
        for lvl in range(1, len(blk)):
            for cn in chains:
                off = jnp.where(blk[lvl] & jnp.logical_not(blk[lvl - 1]), cn["a"], 0.0)
                cn["ot"] = _dot(off.astype(BF16), cn["inv"].astype(BF16))
            for cn in chains:
                cn["inv"] = cn["inv"] - _dot(cn["inv"].astype(BF16), cn["ot"].astype(BF16))
        for cn in chains:
            cn["uw"] = _dot(cn["inv"].astype(BF16), cn["rhs"]).astype(BF16)
        for cn in chains:
            cn["kn"] = _dot_tn(cn["kd"], cn["uw"])
        for cn in chains:
            cn["qo"] = _dot(cn["qkd"], cn["uw"])
        for cn in chains:
            d, ci, rows, hh = cn["d"], cn["ch"]["ci"], cn["ch"]["rows"], cn["ch"]["hh"]
            kq_s[hh, d, ci, 0:GDN_DK, :] = cn["kn"][:, GDN_DK:].astype(BF16)
            kq_s[hh, d, ci, GDN_DK:, :] = (cn["ch"]["qc"] * cn["egbc"] - cn["qo"][:, GDN_DK:]).astype(BF16)
            n_s[hh, d, ci] = cn["kn"][:, 0:GDN_DK]
            obuf(hh, d)[rows, :] = cn["qo"][:, 0:GDN_DK]
            eg_s[hh, d, ci] = jnp.broadcast_to(cn["eglast"], (8, GDN_DK))
        return carry

    lax.fori_loop(0, N_CHUNK // GDN_PREP_CHUNKS, prep, 0)

    def step(t, states):
        ci_rev = jnp.where(t < N_CTX_CHUNK, N_CTX_CHUNK - 1 - t, N_CHUNK + N_CTX_CHUNK - 1 - t)
        cis = (t, ci_rev)
        chains = [(hh, d) for hh in range(GDN_HPS) for d in range(2)]
        rs = [_dot(kq_s[hh, d, cis[d]], states[i].astype(BF16)) for i, (hh, d) in enumerate(chains)]
        new_states = []
        for i, (hh, d) in enumerate(chains):
            rows = pl.ds(pl.multiple_of(cis[d] * c, c), c)
            obuf(hh, d)[rows, :] += rs[i][GDN_DK:, :]
            new_states.append(states[i] * eg_s[hh, d, cis[d]][0:1, :] + n_s[hh, d, cis[d]] - rs[i][0:GDN_DK, :])
        return tuple(new_states)

    zero_state = jnp.zeros((GDN_DK, GDN_DK), F32)
    lax.fori_loop(0, N_CHUNK, step, (zero_state,) * (2 * GDN_HPS))

    for hh in range(GDN_HPS):
        hc = slice(hh * GDN_DK, (hh + 1) * GDN_DK)
        o = o_ref[:, hc] + orev_s[hh]
        y = o * lax.rsqrt(jnp.mean(o * o, axis=-1, keepdims=True) + EPS) * og_ref[...]
        o_ref[:, hc] = y * _silu(z_ref[:, hc])


def _gdn_call(gdn, ab, abt, conv_w, alog_r, dt_r, alog_c, dt_c, onorm_g):
    bsz = gdn.shape[0]

    wblk = GDN_HPS * GDN_DK
    nblk = GDN_H // GDN_HPS

    def head_spec(part):
        return pl.BlockSpec((None, L, wblk), lambda b, h: (b, 0, part * nblk + h), pipeline_mode=pl.Buffered(1))

    def cw_spec(part):
        return pl.BlockSpec((4, wblk), lambda b, h: (0, part * nblk + h))

    def small(shape):
        nd = len(shape)
        return pl.BlockSpec(shape, lambda b, h: (0,) * nd)

    seq = pltpu.VMEM((GDN_HPS, L, GDN_DK), F32)
    return pl.pallas_call(
        _gdn_kernel,
        grid=(bsz, nblk),
        in_specs=[head_spec(0), head_spec(1), head_spec(2), head_spec(3),
                  cw_spec(0), cw_spec(1), cw_spec(2),
                  pl.BlockSpec((None, L, 16), lambda b, h: (b, 0, 0)),
                  pl.BlockSpec((None, N_CHUNK, 16, GDN_C), lambda b, h: (b, 0, 0, 0)),
                  small((1, 16)), small((1, 16)), small((16, 1)), small((16, 1)), small((1, GDN_DK))],
        out_specs=pl.BlockSpec((None, L, wblk), lambda b, h: (b, 0, h)),
        out_shape=jax.ShapeDtypeStruct((bsz, L, GDN_H * GDN_DK), F32),
        scratch_shapes=[seq, seq, seq,
                        pltpu.VMEM((GDN_HPS, 2, N_CHUNK, GDN_DK + GDN_C, GDN_DK), BF16),
                        pltpu.VMEM((GDN_HPS, 2, N_CHUNK, GDN_DK, GDN_DK), F32),
                        pltpu.VMEM((GDN_HPS, 2, N_CHUNK, 8, GDN_DK), F32), seq],
        compiler_params=_cparams(("arbitrary", "arbitrary"), 56),
        name="gdn",
    )(gdn, gdn, gdn, gdn, conv_w, conv_w, conv_w, ab, abt, alog_r, dt_r, alog_c, dt_c, onorm_g)


def _swa_kernel(sink_ref, in_ref, qg_ref, kg_ref, cos_ref, sin_ref, o_ref, q_s, kp_s, vp_s, kc_s, vc_s):
    ii = _iota((128, 128), 0)
    jj = _iota((128, 128), 1)
    head_ones = jnp.where((ii // SWA_DH) == (jj // SWA_DH), 1.0, 0.0).astype(BF16)
    even = (_iota((1, 128), 1) % 2) == 0

    def norm(t, g):
        ssq = _dot((t * t).astype(BF16), head_ones)
        return t * lax.rsqrt(ssq * (1.0 / SWA_DH) + EPS) * g

    def rope(t):
        rot = jnp.where(even, -pltpu.roll(t, 127, 1), pltpu.roll(t, 1, 1))
        return t * cos_ref[...] + rot * sin_ref[...]

    scale = SWA_DH ** -0.5
    for cg in range(SWA_QH * SWA_DH // 128):
        cols = slice(cg * 128, (cg + 1) * 128)
        q_s[0:CTX, cols] = (norm(in_ref[0:CTX, cols], qg_ref[...]) * scale).astype(BF16)
        q_s[CTX:L, cols] = (rope(norm(in_ref[CTX:L, cols], qg_ref[...])) * scale).astype(BF16)
    kcols = slice(512, 640)
    vcols = slice(640, 768)
    zeros_blk = jnp.zeros((SWA_BLK, 128), BF16)
    kc_s[...] = norm(in_ref[0:CTX, kcols], kg_ref[...]).astype(BF16)
    kp_s[0:SWA_BLK, :] = zeros_blk
    kp_s[SWA_BLK:SWA_BLK + SEQ, :] = rope(norm(in_ref[CTX:L, kcols], kg_ref[...])).astype(BF16)
    kp_s[SWA_BLK + SEQ:, :] = zeros_blk
    vc_s[...] = in_ref[0:CTX, vcols].astype(BF16)
    vp_s[0:SWA_BLK, :] = zeros_blk
    vp_s[SWA_BLK:SWA_BLK + SEQ, :] = in_ref[CTX:L, vcols].astype(BF16)
    vp_s[SWA_BLK + SEQ:, :] = zeros_blk

    a_idx = _iota((SWA_BLK, 3 * SWA_BLK), 0)
    s_idx = _iota((SWA_BLK, 3 * SWA_BLK), 1)
    in_window = jnp.abs(s_idx - SWA_BLK - a_idx) <= SWA_BLK

    def latent_block(n, carry):
        q0 = pl.multiple_of(CTX + n * SWA_BLK, SWA_BLK)
        k0 = pl.multiple_of(n * SWA_BLK, SWA_BLK)
        qblk = q_s[pl.ds(q0, SWA_BLK), :]
        kw = kp_s[pl.ds(k0, 3 * SWA_BLK), :]
        vw = vp_s[pl.ds(k0, 3 * SWA_BLK), :]
        kpos = n * SWA_BLK - SWA_BLK + s_idx
        mask = in_window & (kpos >= 0) & (kpos < SEQ)
        heads = range(SWA_QH)
        hsl = [slice(hq * SWA_DH, (hq + 1) * SWA_DH) for hq in heads]
        kvsl = [slice((hq // SWA_GROUP) * SWA_DH, (hq // SWA_GROUP + 1) * SWA_DH) for hq in heads]
        s_loc = [_dot_nt(qblk[:, hsl[hq]], kw[:, kvsl[hq]]) for hq in heads]
        s_ctx = [_dot_nt(qblk[:, hsl[hq]], kc_s[:, kvsl[hq]]) for hq in heads]
        p_loc, p_ctx, den = [], [], []
        for hq in heads:
            sink = sink_ref[hq]
            sl = jnp.where(mask, s_loc[hq], NEG_INF)
            m = jnp.maximum(jnp.maximum(jnp.max(sl, axis=-1, keepdims=True),
                                        jnp.max(s_ctx[hq], axis=-1, keepdims=True)), sink)
            pl_h = jnp.exp(sl - m)
            pc_h = jnp.exp(s_ctx[hq] - m)
            den.append(jnp.sum(pl_h, axis=-1, keepdims=True) + jnp.sum(pc_h, axis=-1, keepdims=True)
                       + jnp.exp(sink - m))
            p_loc.append(pl_h.astype(BF16))
            p_ctx.append(pc_h.astype(BF16))
        acc = [_dot(p_loc[hq], vw[:, kvsl[hq]]) + _dot(p_ctx[hq], vc_s[:, kvsl[hq]]) for hq in heads]
        for hq in heads:
            o_ref[pl.ds(q0, SWA_BLK), hsl[hq]] = acc[hq] / den[hq]
        return carry

    lax.fori_loop(0, SEQ // SWA_BLK, latent_block, 0)

    for n in range(CTX // SWA_BLK):
        rows = slice(n * SWA_BLK, (n + 1) * SWA_BLK)
        qblk = q_s[rows, :]
        for hq in range(SWA_QH):
            kv = hq // SWA_GROUP
            hs = slice(hq * SWA_DH, (hq + 1) * SWA_DH)
            kvs = slice(kv * SWA_DH, (kv + 1) * SWA_DH)
            sink = sink_ref[hq]
            s = _dot_nt(qblk[:, hs], kc_s[:, kvs])
            m = jnp.maximum(jnp.max(s, axis=-1, keepdims=True), sink)
            p = jnp.exp(s - m)
            den = jnp.sum(p, axis=-1, keepdims=True) + jnp.exp(sink - m)
            o_ref[rows, hs] = _dot(p.astype(BF16), vc_s[:, kvs]) / den


def _swa_call(swa, qg, kg, sink, cos_t, sin_t):
    bsz = swa.shape[0]
    wq = SWA_QH * SWA_DH

    def small(shape):
        nd = len(shape)
        return pl.BlockSpec(shape, lambda b: (0,) * nd)

    return pl.pallas_call(
        _swa_kernel,
        grid=(bsz,),
        in_specs=[pl.BlockSpec(memory_space=pltpu.SMEM),
                  pl.BlockSpec((None, L, MAIN_SPLITS[1]), lambda b: (b, 0, 0)),
                  small((1, 128)), small((1, 128)), small((SEQ, 128)), small((SEQ, 128))],
        out_specs=pl.BlockSpec((None, L, wq), lambda b: (b, 0, 0)),
        out_shape=jax.ShapeDtypeStruct((bsz, L, wq), F32),
        scratch_shapes=[pltpu.VMEM((L, wq), BF16),
                        pltpu.VMEM((SEQ + 2 * SWA_BLK, 128), BF16), pltpu.VMEM((SEQ + 2 * SWA_BLK, 128), BF16),
                        pltpu.VMEM((CTX, 128), BF16), pltpu.VMEM((CTX, 128), BF16)],
        compiler_params=_cparams(("arbitrary",), 48),
        name="swa",
    )(sink, swa, qg, kg, cos_t, sin_t)


LRU_RB = 32


def _lru_kernel(in_ref, cw_ref, cb_ref, wg_ref, gb_ref, lam_ref, o_ref, u_s, a_s, b_s):
    row = _iota((L, 1), 0)
    u_s[...] = _seg_conv4(in_ref[:, 0:LRU_W], cw_ref[...], row) + cb_ref[...]

    sub = _iota((LRU_RB, 1), 0) % 8
    n_blk = L // LRU_RB
    n_ctx_blk = CTX // LRU_RB

    for d in range(2):
        sp = _softplus(-lam_ref[d])

        def coeffs(t, carry, d=d, sp=sp):
            rows = pl.ds(pl.multiple_of(t * TM, TM), TM)
            ub = u_s[rows, :]
            pre = _dot(ub.astype(BF16), wg_ref[d]) + gb_ref[d]
            r = _sigmoid(pre[:, 0:LRU_W])
            i = _sigmoid(pre[:, LRU_W:])
            log_a = -LRU_C * r * sp
            a = jnp.exp(log_a)
            a_s[rows, :] = a
            b_s[rows, :] = jnp.sqrt(jnp.tanh(-log_a) * (a * a + 1.0)) * (i * ub)
            return carry

        lax.fori_loop(0, N_TILES, coeffs, 0)

        def scan_block(t, hin, d=d):
            if d == 0:
                blk = t
            else:
                blk = jnp.where(t < n_ctx_blk, n_ctx_blk - 1 - t, n_blk + n_ctx_blk - 1 - t)
            rows = pl.ds(pl.multiple_of(blk * LRU_RB, LRU_RB), LRU_RB)
            a = a_s[rows, :]
            b = b_s[rows, :]
            for s in (1, 2, 4):
                if d == 0:
                    valid = sub >= s
                    shift = s
                else:
                    valid = sub < 8 - s
                    shift = LRU_RB - s
                a_sh = jnp.where(valid, pltpu.roll(a, shift, 0), 1.0)
                b_sh = jnp.where(valid, pltpu.roll(b, shift, 0), 0.0)
                b = a * b_sh + b
                a = a * a_sh
            tiles = range(LRU_RB // 8) if d == 0 else range(LRU_RB // 8 - 1, -1, -1)
            hs = [None] * (LRU_RB // 8)
            for ti in tiles:
                ht = a[ti * 8:(ti + 1) * 8, :] * hin + b[ti * 8:(ti + 1) * 8, :]
                hs[ti] = ht
                hin = ht[7:8, :] if d == 0 else ht[0:1, :]
            hblk = jnp.concatenate(hs, axis=0)
            if d == 0:
                o_ref[rows, :] = hblk
            else:
                o_ref[rows, :] += hblk
            return hin

        lax.fori_loop(0, n_blk, scan_block, jnp.zeros((1, LRU_W), F32))

    y = in_ref[:, LRU_W:]
    gelu = 0.5 * y * (1.0 + jnp.tanh(np.sqrt(2.0 / np.pi) * (y + 0.044715 * (y * y * y))))
    o_ref[...] = o_ref[...] * gelu


def _lru_call(lru, conv_w, conv_b, wg, gb, lam):
    bsz = lru.shape[0]

    def small(shape):
        nd = len(shape)
        return pl.BlockSpec(shape, lambda b: (0,) * nd)

    return pl.pallas_call(
        _lru_kernel,
        grid=(bsz,),
        in_specs=[pl.BlockSpec((None, L, 2 * LRU_W), lambda b: (b, 0, 0)),
                  small((4, LRU_W)), small((1, LRU_W)), small((2, LRU_W, 2 * LRU_W)),
                  small((2, 1, 2 * LRU_W)), small((2, 1, LRU_W))],
        out_specs=pl.BlockSpec((None, L, LRU_W), lambda b: (b, 0, 0)),
        out_shape=jax.ShapeDtypeStruct((bsz, L, LRU_W), F32),
        scratch_shapes=[pltpu.VMEM((L, LRU_W), F32)] * 3,
        compiler_params=_cparams(("arbitrary",), 56),
        name="lru",
    )(lru, conv_w, conv_b, wg, gb, lam)


def _merge_kernel(a_ref, b_ref, c_ref, gl_ref, x_ref, mod_ref, g2_ref, wb_ref, wo_ref, wr_ref,
                  xn_ref, h2_ref, aff_ref):
    mixed = None
    for i, br in enumerate((a_ref, b_ref, c_ref)):
        term = _sigmoid(gl_ref[:, i * D:(i + 1) * D]) * _dot(br[...].astype(BF16), wb_ref[i])
        mixed = term if mixed is None else mixed + term
    mod = mod_ref[...]
    xn = x_ref[...] + mod[2:3] * _dot(mixed.astype(BF16), wo_ref[...])
    xn_ref[...] = xn
    h2 = _modulated_norm(xn, g2_ref[...], mod[3:4], mod[4:5])
    h2_ref[...] = h2.astype(BF16)
    logits = _dot3(h2, wr_ref[...])
    e = jnp.exp(logits - jnp.max(logits, axis=-1, keepdims=True))
    aff_ref[...] = e / jnp.sum(e, axis=-1, keepdims=True)


def _merge_call(a_br, b_br, c_br, gate, xt, modsel, g2, wb, wo, wr):
    bsz = xt.shape[0]
    return pl.pallas_call(
        _merge_kernel,
        grid=(bsz, N_TILES),
        in_specs=[_tile_spec(512), _tile_spec(512), _tile_spec(512), _tile_spec(3 * D), _tile_spec(D),
                  _mod_spec(), _const_spec((1, D)), _const_spec((3, 512, D)), _const_spec((D, D)),
                  _const_spec((D, N_EXP))],
        out_specs=[_tile_spec(D), _tile_spec(D), _tile_spec(N_EXP)],
        out_shape=[jax.ShapeDtypeStruct((bsz, L, D), F32), jax.ShapeDtypeStruct((bsz, L, D), BF16),
                   jax.ShapeDtypeStruct((bsz, L, N_EXP), F32)],
        compiler_params=_cparams(("arbitrary", "arbitrary"), 40),
        name="merge",
    )(a_br, b_br, c_br, gate, xt, modsel, g2, wb, wo, wr)


def _excl_prefix(m):
    w = 256
    pp = _iota((w, w), 0)
    jj = _iota((w, w), 1)
    upper = jnp.where(pp < jj, 1.0, 0.0).astype(BF16)
    off = jnp.zeros((m.shape[0], 1), F32)
    parts = []
    for c0 in range(0, m.shape[1], w):
        blk = m[:, c0:c0 + w]
        parts.append(_dot(blk.astype(BF16), upper) + off)
        off = off + jnp.sum(blk, axis=1, keepdims=True)
    return parts[0] if len(parts) == 1 else jnp.concatenate(parts, axis=1)


def _topk_pos(v, k):
    bits = pltpu.bitcast(v, I32)
    thr = jnp.zeros((v.shape[0], 1), I32)
    for bit in range(30, -1, -1):
        cand = thr | (1 << bit)
        cnt = jnp.sum(jnp.where(bits >= cand, 1.0, 0.0), axis=1, keepdims=True)
        thr = jnp.where(cnt >= k, cand, thr)
    gt = bits > thr
    eq = jnp.where(bits == thr, 1.0, 0.0)
    need = k - jnp.sum(jnp.where(gt, 1.0, 0.0), axis=1, keepdims=True)
    sel = jnp.where(gt, 1.0, jnp.where(_excl_prefix(eq) < need, eq, 0.0))
    return jnp.where(sel > 0.0, _excl_prefix(sel), -1.0).astype(I32)


def _select_kernel(aff_ref, pos_ref):
    pos_ref[:, 0:CTX] = _topk_pos(aff_ref[:, 0:CTX], CAP_C)
    pos_ref[:, CTX:L] = _topk_pos(aff_ref[:, CTX:L], CAP_X)


def _select_call(aff_t):
    bsz = aff_t.shape[0]
    return pl.pallas_call(
        _select_kernel,
        grid=(bsz,),
        in_specs=[pl.BlockSpec((None, N_EXP, L), lambda b: (b, 0, 0))],
        out_specs=pl.BlockSpec((None, N_EXP, L), lambda b: (b, 0, 0)),
        out_shape=jax.ShapeDtypeStruct((bsz, N_EXP, L), I32),
        compiler_params=_cparams(("arbitrary",), 32),
        name="select",
    )(aff_t)


def _gather_kernel(pos_ref, aff_ref, h_ref, xs_ref, xc_ref, gx_ref, gc_ref):
    pos = pos_ref[...]
    aff = aff_ref[...]
    for lo, hi, cap, x_ref, g_ref in ((CTX, L, CAP_X, xs_ref, gx_ref), (0, CTX, CAP_C, xc_ref, gc_ref)):
        hit = _iota((cap, hi - lo), 0) == pos[:, lo:hi]
        x_ref[...] = _dot(jnp.where(hit, 1.0, 0.0).astype(BF16), h_ref[lo:hi, :]).astype(BF16)
        g_ref[...] = jnp.sum(jnp.where(hit, aff[:, lo:hi], 0.0), axis=1, keepdims=True)


def _gather_call(pos4, aff4, h2):
    bsz = h2.shape[0]
    row_spec = pl.BlockSpec((None, None, 1, L), lambda b, e: (b, e, 0, 0))

    def cap_spec(cap, width):
        return pl.BlockSpec((None, cap, width), lambda b, e: (e, b, 0))

    return pl.pallas_call(
        _gather_kernel,
        grid=(bsz, N_EXP),
        in_specs=[row_spec, row_spec, pl.BlockSpec((None, L, D), lambda b, e: (b, 0, 0))],
        out_specs=[cap_spec(CAP_X, D), cap_spec(CAP_C, D), cap_spec(CAP_X, 1), cap_spec(CAP_C, 1)],
        out_shape=[jax.ShapeDtypeStruct((N_EXP, bsz * CAP_X, D), BF16),
                   jax.ShapeDtypeStruct((N_EXP, bsz * CAP_C, D), BF16),
                   jax.ShapeDtypeStruct((N_EXP, bsz * CAP_X, 1), F32),
                   jax.ShapeDtypeStruct((N_EXP, bsz * CAP_C, 1), F32)],
        compiler_params=_cparams(("arbitrary", "arbitrary"), 40),
        name="gather",
    )(pos4, aff4, h2)


FFN_HC = 512
FFN_MT = 512


def _ffn_kernel(xs_ref, xc_ref, gx_ref, gc_ref, wg_ref, wu_ref, wd_ref, ys_ref, yc_ref,
                acc_x, acc_c, wgb, wub, wdb):
    hc = pl.program_id(1)
    wgb[...] = wg_ref[...].astype(BF16)
    wub[...] = wu_ref[...].astype(BF16)
    wdb[...] = wd_ref[...].astype(BF16)

    def ffn(x):
        g = _dot(x, wgb[...])
        hid = _silu(g) * _dot(x, wub[...])
        return _dot(hid.astype(BF16), wdb[...])

    @pl.when(hc == 0)
    def _():
        acc_x[...] = jnp.zeros_like(acc_x)
        acc_c[...] = jnp.zeros_like(acc_c)

    for m0 in range(0, xs_ref.shape[0], FFN_MT):
        acc_x[m0:m0 + FFN_MT, :] += ffn(xs_ref[m0:m0 + FFN_MT, :])
    acc_c[...] += ffn(xc_ref[...])

    @pl.when(hc == pl.num_programs(1) - 1)
    def _():
        ys_ref[...] = (acc_x[...] * gx_ref[...]).astype(BF16)
        yc_ref[...] = (acc_c[...] * gc_ref[...]).astype(BF16)


def _ffn_call(xs, xc, gx, gc, w_gate, w_up, w_down, layer):
    mx, mc = xs.shape[1], xc.shape[1]
    return pl.pallas_call(
        _ffn_kernel,
        grid=(N_EXP, EXP_HID // FFN_HC),
        in_specs=[pl.BlockSpec((None, mx, D), lambda e, j: (e, 0, 0)),
                  pl.BlockSpec((None, mc, D), lambda e, j: (e, 0, 0)),
                  pl.BlockSpec((None, mx, 1), lambda e, j: (e, 0, 0)),
                  pl.BlockSpec((None, mc, 1), lambda e, j: (e, 0, 0)),
                  pl.BlockSpec((None, None, D, FFN_HC), lambda e, j: (layer, e, 0, j)),
                  pl.BlockSpec((None, None, D, FFN_HC), lambda e, j: (layer, e, 0, j)),
                  pl.BlockSpec((None, None, FFN_HC, D), lambda e, j: (layer, e, j, 0))],
        out_specs=[pl.BlockSpec((None, mx, D), lambda e, j: (e, 0, 0)),
                   pl.BlockSpec((None, mc, D), lambda e, j: (e, 0, 0))],
        out_shape=[jax.ShapeDtypeStruct(xs.shape, BF16), jax.ShapeDtypeStruct(xc.shape, BF16)],
        scratch_shapes=[pltpu.VMEM((mx, D), F32), pltpu.VMEM((mc, D), F32),
                        pltpu.VMEM((D, FFN_HC), BF16), pltpu.VMEM((D, FFN_HC), BF16),
                        pltpu.VMEM((FFN_HC, D), BF16)],
        compiler_params=_cparams(("arbitrary", "arbitrary"), 56),
        name="ffn",
    )(xs, xc, gx, gc, w_gate, w_up, w_down)


def _scatter_kernel(ys_ref, yc_ref, pos_ref, x_ref, g2_ref, o_ref):
    t = pl.program_id(1)
    pos = pos_ref[...]

    def scattered(y_ref, cap):
        ranks = _iota((TM, cap), 1)
        onehot = jnp.concatenate(
            [jnp.where(pos[:, e:e + 1] == ranks, 1.0, 0.0).astype(BF16) for e in range(N_EXP)], axis=1)
        return _dot(onehot, y_ref[...].reshape(N_EXP * cap, D))

    @pl.when(t == 0)
    def _():
        o_ref[...] = x_ref[...] + g2_ref[...] * scattered(yc_ref, CAP_C)

    @pl.when(t > 0)
    def _():
        o_ref[...] = x_ref[...] + g2_ref[...] * scattered(ys_ref, CAP_X)


def _scatter_call(ys, yc, pos_t, xn, gate2):
    bsz = xn.shape[0]
    return pl.pallas_call(
        _scatter_kernel,
        grid=(bsz, N_TILES),
        in_specs=[pl.BlockSpec((N_EXP, CAP_X, D), lambda b, t: (0, b, 0)),
                  pl.BlockSpec((N_EXP, CAP_C, D), lambda b, t: (0, b, 0)),
                  _tile_spec(N_EXP), _tile_spec(D),
                  pl.BlockSpec((None, None, 1, D), lambda b, t: (b, jnp.minimum(t, 1), 0, 0))],
        out_specs=_tile_spec(D),
        out_shape=jax.ShapeDtypeStruct((bsz, L, D), F32),
        compiler_params=_cparams(("arbitrary", "arbitrary"), 40),
        name="scatter",
    )(ys, yc, pos_t, xn, gate2)


def _rope_tables():
    rows = SEQ // GRID_W
    row = jnp.repeat(jnp.arange(rows, dtype=F32), GRID_W)
    col = jnp.tile(jnp.arange(GRID_W, dtype=F32), rows)
    axis_dim = SWA_DH // 2
    inv_freq = ROPE_THETA ** (-jnp.arange(0, axis_dim, 2, dtype=F32) / axis_dim)
    ang = jnp.concatenate([row[:, None] * inv_freq, col[:, None] * inv_freq], axis=-1)
    ang = jnp.tile(jnp.repeat(ang, 2, axis=-1), (1, 128 // SWA_DH))
    return jnp.cos(ang), jnp.sin(ang)


def _block_diag(w):
    n, bw, _ = w.shape
    eye = jnp.eye(n, dtype=w.dtype)
    return (eye[:, None, :, None] * w[:, :, None, :]).reshape(n * bw, n * bw)


def _pad16(v):
    return jnp.concatenate([v.reshape(-1), jnp.zeros((16 - v.size,), v.dtype)])


def _layer(xt, mods_l, layer, norm1_g, norm2_g, w_in, gdn_conv_w, gdn_a_log, gdn_dt_bias, gdn_onorm_g,
           swa_qnorm_g, swa_knorm_g, swa_sink, lru_conv_w, lru_conv_b, lru_gate_w, lru_gate_b, lru_lambda,
           w_branch, w_out, w_router, w_exp_gate, w_exp_up, w_exp_down, cos_t, sin_t):
    bsz = xt.shape[0]
    mods6 = mods_l.reshape(16, 6, D)
    modsel = jnp.stack([jnp.broadcast_to(mods6[8], (bsz, 6, D)), mods6[:bsz]], axis=1)

    w_main = jnp.concatenate([w_in[:, :2048], w_in[:, 2064:]], axis=1).astype(BF16)
    w_ab = w_in[:, 2048:2064].astype(BF16)
    gdn, swa, lru, gate, ab = _inproj_call(xt, modsel, norm1_g.reshape(1, D), w_main, w_ab)

    abt = jnp.swapaxes(ab.reshape(bsz, N_CHUNK, GDN_C, 16), 2, 3)
    alog16, dt16 = _pad16(gdn_a_log), _pad16(gdn_dt_bias)
    a_br = _gdn_call(gdn, ab, abt, gdn_conv_w, alog16.reshape(1, 16), dt16.reshape(1, 16),
                     alog16.reshape(16, 1), dt16.reshape(16, 1), gdn_onorm_g.reshape(1, GDN_DK))

    b_br = _swa_call(swa, jnp.tile(swa_qnorm_g, 128 // SWA_DH).reshape(1, 128),
                     jnp.tile(swa_knorm_g, 128 // SWA_DH).reshape(1, 128), swa_sink, cos_t, sin_t)

    wg = jnp.stack([jnp.concatenate([_block_diag(lru_gate_w[d, 0]), _block_diag(lru_gate_w[d, 1])], axis=1)
                    for d in range(2)]).astype(BF16)
    c_br = _lru_call(lru, lru_conv_w, lru_conv_b.reshape(1, LRU_W), wg,
                     lru_gate_b.reshape(2, 1, 2 * LRU_W), lru_lambda.reshape(2, 1, LRU_W))

    xn, h2, aff = _merge_call(a_br, b_br, c_br, gate, xt, modsel, norm2_g.reshape(1, D),
                              w_branch.astype(BF16), w_out.astype(BF16), w_router)

    aff_t = jnp.swapaxes(aff, 1, 2)
    pos = _select_call(aff_t)
    xs, xc, gx, gc = _gather_call(pos.reshape(bsz, N_EXP, 1, L), aff_t.reshape(bsz, N_EXP, 1, L), h2)
    ys, yc = _ffn_call(xs, xc, gx, gc, w_exp_gate, w_exp_up, w_exp_down, layer)
    gate2 = modsel[:, :, 5:6, :]
    return _scatter_call(ys, yc, jnp.swapaxes(pos, 1, 2), xn, gate2)


def kernel(x, c, ctx, c_ctx, w_ada, b_ada, norm1_g, norm2_g, w_in, gdn_conv_w, gdn_a_log, gdn_dt_bias, gdn_onorm_g, swa_qnorm_g, swa_knorm_g, swa_sink, lru_conv_w, lru_conv_b, lru_gate_w, lru_gate_b, lru_lambda, w_branch, w_out, w_router, w_exp_gate, w_exp_up, w_exp_down):
    bsz = x.shape[0]
    depth = w_ada.shape[0]
    assert x.shape == (bsz, SEQ, D) and ctx.shape == (bsz, CTX, D) and bsz <= 8
    cond16 = jnp.concatenate([c, jnp.zeros((8 - bsz, D), F32), c_ctx[None, :], jnp.zeros((7, D), F32)], axis=0)
    mods = _ada_call(cond16, w_ada, b_ada)
    cos_t, sin_t = _rope_tables()
    xt = jnp.concatenate([ctx, x], axis=1)
    for l in range(depth):
        xt = _layer(xt, mods[l], l, norm1_g[l], norm2_g[l], w_in[l], gdn_conv_w[l], gdn_a_log[l],
                    gdn_dt_bias[l], gdn_onorm_g[l], swa_qnorm_g[l], swa_knorm_g[l], swa_sink[l],
                    lru_conv_w[l], lru_conv_b[l], lru_gate_w[l], lru_gate_b[l], lru_lambda[l],
                    w_branch[l], w_out[l], w_router[l], w_exp_gate, w_exp_up, w_exp_down, cos_t, sin_t)
    return xt[:, CTX:, :]
```

```python
import functools

import jax
import jax.numpy as jnp
import numpy as np
from jax import lax
from jax.experimental import pallas as pl
from jax.experimental.pallas import tpu as pltpu

F32 = jnp.float32
BF16 = jnp.bfloat16
I32 = jnp.int32

D = 1024
SEQ = 2048
CTX = 256
L = CTX + SEQ
GRID_W = 64
EPS = 1e-6
NEG_INF = -1e30
ROPE_THETA = 10000.0

GDN_H = 4
GDN_DK = 128
GDN_C = 64
N_CHUNK = L // GDN_C
N_CTX_CHUNK = CTX // GDN_C

SWA_QH = 8
SWA_KVH = 2
SWA_GROUP = SWA_QH // SWA_KVH
SWA_DH = 64
SWA_BLK = 128

LRU_W = 512
LRU_NBLK = 8
LRU_C = 8.0

N_EXP = 16
EXP_HID = 2048
CAP_X = 2 * SEQ // N_EXP
CAP_C = 2 * CTX // N_EXP

TM = 256
N_TILES = L // TM
MAIN_SPLITS = (2048, 768, 1024, 3072)
N_MAIN = sum(MAIN_SPLITS)

VMEM_MIB_V7X = 64


def _cparams(sem, vmem_mib):
    assert vmem_mib < VMEM_MIB_V7X
    return pltpu.CompilerParams(dimension_semantics=sem, vmem_limit_bytes=vmem_mib * 1024 * 1024)


def _dot(a, b):
    return jnp.dot(a, b, preferred_element_type=F32)


def _dot_nt(a, b):
    return lax.dot_general(a, b, (((1,), (1,)), ((), ())), preferred_element_type=F32)


def _dot_tn(a, b):
    return lax.dot_general(a, b, (((0,), (0,)), ((), ())), preferred_element_type=F32)


def _split(x):
    hi = x.astype(BF16)
    lo = (x - hi.astype(F32)).astype(BF16)
    return hi, lo


def _dot3(a, b):
    ah, al = _split(a)
    bh, bl = _split(b)
    return _dot(ah, bh) + _dot(al, bh) + _dot(ah, bl)


def _sigmoid(x):
    return 0.5 * jnp.tanh(0.5 * x) + 0.5


def _silu(x):
    return x * _sigmoid(x)


def _softplus(x):
    return jnp.maximum(x, 0.0) + jnp.log1p(jnp.exp(-jnp.abs(x)))


def _iota(shape, dim):
    return lax.broadcasted_iota(I32, shape, dim)


def _seg_conv4(x, w, row):
    n = x.shape[0]
    segpos = jnp.where(row < CTX, row, row - CTX)
    is_last = (row == CTX - 1) | (row == n - 1)
    xm2 = jnp.where(segpos >= 2, pltpu.roll(x, 2, 0), 0.0)
    xm1 = jnp.where(segpos >= 1, pltpu.roll(x, 1, 0), 0.0)
    xp1 = jnp.where(is_last, 0.0, pltpu.roll(x, n - 1, 0))
    return xm2 * w[0:1] + xm1 * w[1:2] + x * w[2:3] + xp1 * w[3:4]


def _ada_kernel(c_ref, w_ref, b_ref, o_ref):
    o_ref[...] = _dot3(_silu(c_ref[...]), w_ref[...]) + b_ref[...]


def _ada_call(cond16, w_ada, b_ada):
    depth, _, n = w_ada.shape
    tn = 1536
    return pl.pallas_call(
        _ada_kernel,
        grid=(depth, n // tn),
        in_specs=[pl.BlockSpec((16, D), lambda l, j: (0, 0)),
                  pl.BlockSpec((None, D, tn), lambda l, j: (l, 0, j)),
                  pl.BlockSpec((None, 1, tn), lambda l, j: (l, 0, j))],
        out_specs=pl.BlockSpec((None, 16, tn), lambda l, j: (l, 0, j)),
        out_shape=jax.ShapeDtypeStruct((depth, 16, n), F32),
        compiler_params=_cparams(("arbitrary", "arbitrary"), 32),
        name="ada",
    )(cond16, w_ada, b_ada.reshape(depth, 1, n))


def _modulated_norm(x, g, shift, scale):
    xn = x * lax.rsqrt(jnp.mean(x * x, axis=-1, keepdims=True) + EPS) * g
    return xn * (1.0 + scale) + shift


def _inproj_kernel(x_ref, mod_ref, g_ref, wm_ref, wab_ref, gdn_ref, swa_ref, lru_ref, gate_ref, ab_ref):
    mod = mod_ref[...]
    hb = _modulated_norm(x_ref[...], g_ref[...], mod[0:1], mod[1:2]).astype(BF16)
    col = 0
    for ref, width in zip((gdn_ref, swa_ref, lru_ref, gate_ref), MAIN_SPLITS):
        for c0 in range(0, width, 512):
            cw = min(512, width - c0)
            ref[:, c0:c0 + cw] = _dot(hb, wm_ref[:, col + c0:col + c0 + cw])
        col += width
    ab_ref[...] = _dot(hb, wab_ref[...])


def _mod_spec():
    return pl.BlockSpec((None, None, 6, D), lambda b, t: (b, jnp.minimum(t, 1), 0, 0))


def _tile_spec(width):
    return pl.BlockSpec((None, TM, width), lambda b, t: (b, t, 0))


def _const_spec(shape):
    nd = len(shape)
    return pl.BlockSpec(shape, lambda b, t: (0,) * nd, pipeline_mode=pl.Buffered(1))


def _inproj_call(xt, modsel, g1, w_main, w_ab):
    bsz = xt.shape[0]
    outs = [jax.ShapeDtypeStruct((bsz, L, w), F32) for w in MAIN_SPLITS] + [jax.ShapeDtypeStruct((bsz, L, 16), F32)]
    return pl.pallas_call(
        _inproj_kernel,
        grid=(bsz, N_TILES),
        in_specs=[_tile_spec(D), _mod_spec(), _const_spec((1, D)),
                  _const_spec((D, N_MAIN)), _const_spec((D, 16))],
        out_specs=[_tile_spec(w) for w in MAIN_SPLITS] + [_tile_spec(16)],
        out_shape=outs,
        compiler_params=_cparams(("arbitrary", "arbitrary"), 48),
        name="inproj",
    )(xt, modsel, g1, w_main, w_ab)


GDN_HPS = 2
GDN_PREP_CHUNKS = 6


def _gdn_kernel(q_ref, k_ref, v_ref, z_ref, cwq_ref, cwk_ref, cwv_ref, ab_ref, abt_ref,
                alog_r_ref, dt_r_ref, alog_c_ref, dt_c_ref, og_ref, o_ref,
                qs, ks, vs, kq_s, n_s, eg_s, orev_s):
    head0 = pl.program_id(1) * GDN_HPS
    row = _iota((L, 1), 0)

    for hh in range(GDN_HPS):
        hc = slice(hh * GDN_DK, (hh + 1) * GDN_DK)
        q = _silu(_seg_conv4(q_ref[:, hc], cwq_ref[:, hc], row))
        qs[hh] = q * lax.rsqrt(jnp.sum(q * q, axis=-1, keepdims=True) + EPS) * (GDN_DK ** -0.5)
        k = _silu(_seg_conv4(k_ref[:, hc], cwk_ref[:, hc], row))
        ks[hh] = k * lax.rsqrt(jnp.sum(k * k, axis=-1, keepdims=True) + EPS)
        vs[hh] = _silu(_seg_conv4(v_ref[:, hc], cwv_ref[:, hc], row))

    c = GDN_C
    ii = _iota((c, c), 0)
    jj = _iota((c, c), 1)
    eye = jnp.where(ii == jj, 1.0, 0.0)
    blk = [(ii // b) == (jj // b) for b in (8, 16, 32, 64)]
    lane16 = _iota((c, 16), 1)
    sub16 = _iota((16, c), 0)

    def obuf(hh, d):
        return o_ref.at[:, hh * GDN_DK:(hh + 1) * GDN_DK] if d == 0 else orev_s.at[hh]

    def prep(it, carry):
        chunks = []
        for u in range(GDN_PREP_CHUNKS):
            ci = it * GDN_PREP_CHUNKS + u
            rows = pl.ds(pl.multiple_of(ci * c, c), c)
            ab = ab_ref[rows, :]
            g_all = -jnp.exp(alog_r_ref[...]) * _softplus(ab + dt_r_ref[...])
            beta_all = _sigmoid(ab)
            g_all_t = -jnp.exp(alog_c_ref[...]) * _softplus(abt_ref[ci] + dt_c_ref[...])
            for hh in range(GDN_HPS):
                qc, kc, vc = qs[hh, rows, :], ks[hh, rows, :], vs[hh, rows, :]
                chunks.append(dict(ci=ci, rows=rows, hh=hh, qc=qc, kc=kc, vc=vc, g_all=g_all, beta_all=beta_all,
                                   g_all_t=g_all_t, kcb=kc.astype(BF16), qcb=qc.astype(BF16)))
        for ch in chunks:
            ch["kk"] = _dot_nt(ch["kcb"], ch["kcb"])
        for ch in chunks:
            ch["qk"] = _dot_nt(ch["qcb"], ch["kcb"])
        chains = []
        for ch in chunks:
            g_all, beta_all, g_all_t = ch["g_all"], ch["beta_all"], ch["g_all_t"]
            for d in range(2):
                if d == 0:
                    incl, strict, incl_t = jj <= ii, jj < ii, ii <= jj
                else:
                    incl, strict, incl_t = jj >= ii, jj > ii, ii >= jj
                a_idx = d * GDN_H + head0 + ch["hh"]
                gcol = jnp.sum(jnp.where(lane16 == a_idx, g_all, 0.0), axis=1, keepdims=True)
                bcol = jnp.sum(jnp.where(lane16 == 2 * GDN_H + a_idx, beta_all, 0.0), axis=1, keepdims=True)
                grow = jnp.sum(jnp.where(sub16 == a_idx, g_all_t, 0.0), axis=0, keepdims=True)
                gbc = jnp.sum(jnp.where(incl, grow, 0.0), axis=1, keepdims=True)
                gbr = jnp.sum(jnp.where(incl_t, gcol, 0.0), axis=0, keepdims=True)
                decay = jnp.where(incl, jnp.exp(jnp.where(incl, gbc - gbr, 0.0)), 0.0)
                a_mat = jnp.where(strict, ch["kk"] * bcol * decay, 0.0)
                egbc = jnp.exp(gbc)
                glast = gbc[c - 1:c, :] if d == 0 else gbc[0:1, :]
                chains.append(dict(
                    ch=ch, d=d, a=a_mat, egbc=egbc, eglast=jnp.exp(glast),
                    rhs=jnp.concatenate([ch["vc"] * bcol, ch["kc"] * (bcol * egbc)], axis=1).astype(BF16),
                    kd=(ch["kc"] * jnp.exp(glast - gbc)).astype(BF16),
                    qkd=(ch["qk"] * decay).astype(BF16)))
        for cn in chains:
            a8 = jnp.where(blk[0], cn["a"], 0.0)
            a8b = a8.astype(BF16)
            cn["p2"] = _dot(a8b, a8b)
            cn["ima"] = (eye - a8).astype(BF16)
        for cn in chains:
            p2b = cn["p2"].astype(BF16)
            cn["p4"] = _dot(p2b, p2b)
            cn["inv"] = _dot(cn["ima"], (eye + cn["p2"]).astype(BF16))
        for cn in chains:
            cn["inv"] = _dot(cn["inv"].astype(BF16), (eye + cn["p4"]).astype(BF16))
        for lvl in range(1, len(blk)):
            for cn in chains:
                off = jnp.where(blk[lvl] & jnp.logical_not(blk[lvl - 1]), cn["a"], 0.0)
                cn["ot"] = _dot(off.astype(BF16), cn["inv"].astype(BF16))
            for cn in chains:
                cn["inv"] = cn["inv"] - _dot(cn["inv"].astype(BF16), cn["ot"].astype(BF16))
        for cn in chains:
            cn["uw"] = _dot(cn["inv"].astype(BF16), cn["rhs"]).astype(BF16)
        for cn in chains:
            cn["kn"] = _dot_tn(cn["kd"], cn["uw"])
        for cn in chains:
            cn["qo"] = _dot(cn["qkd"], cn["uw"])
        for cn in chains:
            d, ci, rows, hh = cn["d"], cn["ch"]["ci"], cn["ch"]["rows"], cn["ch"]["hh"]
            kq_s[hh, d, ci, 0:GDN_DK, :] = cn["kn"][:, GDN_DK:].astype(BF16)
            kq_s[hh, d, ci, GDN_DK:, :] = (cn["ch"]["qc"] * cn["egbc"] - cn["qo"][:, GDN_DK:]).astype(BF16)
            n_s[hh, d, ci] = cn["kn"][:, 0:GDN_DK]
            obuf(hh, d)[rows, :] = cn["qo"][:, 0:GDN_DK]
            eg_s[hh, d, ci] = jnp.broadcast_to(cn["eglast"], (8, GDN_DK))
        return carry

    lax.fori_loop(0, N_CHUNK // GDN_PREP_CHUNKS, prep, 0)

    def step(t, states):
        ci_rev = jnp.where(t < N_CTX_CHUNK, N_CTX_CHUNK - 1 - t, N_CHUNK + N_CTX_CHUNK - 1 - t)
        cis = (t, ci_rev)
        chains = [(hh, d) for hh in range(GDN_HPS) for d in range(2)]
        rs = [_dot(kq_s[hh, d, cis[d]], states[i].astype(BF16)) for i, (hh, d) in enumerate(chains)]
        new_states = []
        for i, (hh, d) in enumerate(chains):
            rows = pl.ds(pl.multiple_of(cis[d] * c, c), c)
            obuf(hh, d)[rows, :] += rs[i][GDN_DK:, :]
            new_states.append(states[i] * eg_s[hh, d, cis[d]][0:1, :] + n_s[hh, d, cis[d]] - rs[i][0:GDN_DK, :])
        return tuple(new_states)

    zero_state = jnp.zeros((GDN_DK, GDN_DK), F32)
    lax.fori_loop(0, N_CHUNK, step, (zero_state,) * (2 * GDN_HPS))

    for hh in range(GDN_HPS):
        hc = slice(hh * GDN_DK, (hh + 1) * GDN_DK)
        o = o_ref[:, hc] + orev_s[hh]
        y = o * lax.rsqrt(jnp.mean(o * o, axis=-1, keepdims=True) + EPS) * og_ref[...]
        o_ref[:, hc] = y * _silu(z_ref[:, hc])


def _gdn_call(gdn, ab, abt, conv_w, alog_r, dt_r, alog_c, dt_c, onorm_g):
    bsz = gdn.shape[0]

    wblk = GDN_HPS * GDN_DK
    nblk = GDN_H // GDN_HPS

    def head_spec(part):
        return pl.BlockSpec((None, L, wblk), lambda b, h: (b, 0, part * nblk + h), pipeline_mode=pl.Buffered(1))

    def cw_spec(part):
        return pl.BlockSpec((4, wblk), lambda b, h: (0, part * nblk + h))

    def small(shape):
        nd = len(shape)
        return pl.BlockSpec(shape, lambda b, h: (0,) * nd)

    seq = pltpu.VMEM((GDN_HPS, L, GDN_DK), F32)
    return pl.pallas_call(
        _gdn_kernel,
        grid=(bsz, nblk),
        in_specs=[head_spec(0), head_spec(1), head_spec(2), head_spec(3),
                  cw_spec(0), cw_spec(1), cw_spec(2),
                  pl.BlockSpec((None, L, 16), lambda b, h: (b, 0, 0)),
                  pl.BlockSpec((None, N_CHUNK, 16, GDN_C), lambda b, h: (b, 0, 0, 0)),
                  small((1, 16)), small((1, 16)), small((16, 1)), small((16, 1)), small((1, GDN_DK))],
        out_specs=pl.BlockSpec((None, L, wblk), lambda b, h: (b, 0, h)),
        out_shape=jax.ShapeDtypeStruct((bsz, L, GDN_H * GDN_DK), F32),
        scratch_shapes=[seq, seq, seq,
                        pltpu.VMEM((GDN_HPS, 2, N_CHUNK, GDN_DK + GDN_C, GDN_DK), BF16),
                        pltpu.VMEM((GDN_HPS, 2, N_CHUNK, GDN_DK, GDN_DK), F32),
                        pltpu.VMEM((GDN_HPS, 2, N_CHUNK, 8, GDN_DK), F32), seq],
        compiler_params=_cparams(("arbitrary", "arbitrary"), 56),
        name="gdn",
    )(gdn, gdn, gdn, gdn, conv_w, conv_w, conv_w, ab, abt, alog_r, dt_r, alog_c, dt_c, onorm_g)


def _swa_kernel(sink_ref, in_ref, qg_ref, kg_ref, cos_ref, sin_ref, o_ref, q_s, kp_s, vp_s, kc_s, vc_s):
    ii = _iota((128, 128), 0)
    jj = _iota((128, 128), 1)
    head_ones = jnp.where((ii // SWA_DH) == (jj // SWA_DH), 1.0, 0.0).astype(BF16)
    even = (_iota((1, 128), 1) % 2) == 0

    def norm(t, g):
        ssq = _dot((t * t).astype(BF16), head_ones)
        return t * lax.rsqrt(ssq * (1.0 / SWA_DH) + EPS) * g

    def rope(t):
        rot = jnp.where(even, -pltpu.roll(t, 127, 1), pltpu.roll(t, 1, 1))
        return t * cos_ref[...] + rot * sin_ref[...]

    scale = SWA_DH ** -0.5
    for cg in range(SWA_QH * SWA_DH // 128):
        cols = slice(cg * 128, (cg + 1) * 128)
        q_s[0:CTX, cols] = (norm(in_ref[0:CTX, cols], qg_ref[...]) * scale).astype(BF16)
        q_s[CTX:L, cols] = (rope(norm(in_ref[CTX:L, cols], qg_ref[...])) * scale).astype(BF16)
    kcols = slice(512, 640)
    vcols = slice(640, 768)
    zeros_blk = jnp.zeros((SWA_BLK, 128), BF16)
    kc_s[...] = norm(in_ref[0:CTX, kcols], kg_ref[...]).astype(BF16)
    kp_s[0:SWA_BLK, :] = zeros_blk
    kp_s[SWA_BLK:SWA_BLK + SEQ, :] = rope(norm(in_ref[CTX:L, kcols], kg_ref[...])).astype(BF16)
    kp_s[SWA_BLK + SEQ:, :] = zeros_blk
    vc_s[...] = in_ref[0:CTX, vcols].astype(BF16)
    vp_s[0:SWA_BLK, :] = zeros_blk
    vp_s[SWA_BLK:SWA_BLK + SEQ, :] = in_ref[CTX:L, vcols].astype(BF16)
    vp_s[SWA_BLK + SEQ:, :] = zeros_blk

    a_idx = _iota((SWA_BLK, 3 * SWA_BLK), 0)
    s_idx = _iota((SWA_BLK, 3 * SWA_BLK), 1)
    in_window = jnp.abs(s_idx - SWA_BLK - a_idx) <= SWA_BLK

    def latent_block(n, carry):
        q0 = pl.multiple_of(CTX + n * SWA_BLK, SWA_BLK)
        k0 = pl.multiple_of(n * SWA_BLK, SWA_BLK)
        qblk = q_s[pl.ds(q0, SWA_BLK), :]
        kw = kp_s[pl.ds(k0, 3 * SWA_BLK), :]
        vw = vp_s[pl.ds(k0, 3 * SWA_BLK), :]
        kpos = n * SWA_BLK - SWA_BLK + s_idx
        mask = in_window & (kpos >= 0) & (kpos < SEQ)
        heads = range(SWA_QH)
        hsl = [slice(hq * SWA_DH, (hq + 1) * SWA_DH) for hq in heads]
        kvsl = [slice((hq // SWA_GROUP) * SWA_DH, (hq // SWA_GROUP + 1) * SWA_DH) for hq in heads]
        s_loc = [_dot_nt(qblk[:, hsl[hq]], kw[:, kvsl[hq]]) for hq in heads]
        s_ctx = [_dot_nt(qblk[:, hsl[hq]], kc_s[:, kvsl[hq]]) for hq in heads]
        p_loc, p_ctx, den = [], [], []
        for hq in heads:
            sink = sink_ref[hq]
            sl = jnp.where(mask, s_loc[hq], NEG_INF)
            m = jnp.maximum(jnp.maximum(jnp.max(sl, axis=-1, keepdims=True),
                                        jnp.max(s_ctx[hq], axis=-1, keepdims=True)), sink)
            pl_h = jnp.exp(sl - m)
            pc_h = jnp.exp(s_ctx[hq] - m)
            den.append(jnp.sum(pl_h, axis=-1, keepdims=True) + jnp.sum(pc_h, axis=-1, keepdims=True)
                       + jnp.exp(sink - m))
            p_loc.append(pl_h.astype(BF16))
            p_ctx.append(pc_h.astype(BF16))
        acc = [_dot(p_loc[hq], vw[:, kvsl[hq]]) + _dot(p_ctx[hq], vc_s[:, kvsl[hq]]) for hq in heads]
        for hq in heads:
            o_ref[pl.ds(q0, SWA_BLK), hsl[hq]] = acc[hq] / den[hq]
        return carry

    lax.fori_loop(0, SEQ // SWA_BLK, latent_block, 0)

    for n in range(CTX // SWA_BLK):
        rows = slice(n * SWA_BLK, (n + 1) * SWA_BLK)
        qblk = q_s[rows, :]
        for hq in range(SWA_QH):
            kv = hq // SWA_GROUP
            hs = slice(hq * SWA_DH, (hq + 1) * SWA_DH)
            kvs = slice(kv * SWA_DH, (kv + 1) * SWA_DH)
            sink = sink_ref[hq]
            s = _dot_nt(qblk[:, hs], kc_s[:, kvs])
            m = jnp.maximum(jnp.max(s, axis=-1, keepdims=True), sink)
            p = jnp.exp(s - m)
            den = jnp.sum(p, axis=-1, keepdims=True) + jnp.exp(sink - m)
            o_ref[rows, hs] = _dot(p.astype(BF16), vc_s[:, kvs]) / den


def _swa_call(swa, qg, kg, sink, cos_t, sin_t):
    bsz = swa.shape[0]
    wq = SWA_QH * SWA_DH

    def small(shape):
        nd = len(shape)
        return pl.BlockSpec(shape, lambda b: (0,) * nd)

    return pl.pallas_call(
        _swa_kernel,
        grid=(bsz,),
        in_specs=[pl.BlockSpec(memory_space=pltpu.SMEM),
                  pl.BlockSpec((None, L, MAIN_SPLITS[1]), lambda b: (b, 0, 0)),
                  small((1, 128)), small((1, 128)), small((SEQ, 128)), small((SEQ, 128))],
        out_specs=pl.BlockSpec((None, L, wq), lambda b: (b, 0, 0)),
        out_shape=jax.ShapeDtypeStruct((bsz, L, wq), F32),
        scratch_shapes=[pltpu.VMEM((L, wq), BF16),
                        pltpu.VMEM((SEQ + 2 * SWA_BLK, 128), BF16), pltpu.VMEM((SEQ + 2 * SWA_BLK, 128), BF16),
                        pltpu.VMEM((CTX, 128), BF16), pltpu.VMEM((CTX, 128), BF16)],
        compiler_params=_cparams(("arbitrary",), 48),
        name="swa",
    )(sink, swa, qg, kg, cos_t, sin_t)


LRU_RB = 32


def _lru_kernel(in_ref, cw_ref, cb_ref, wg_ref, gb_ref, lam_ref, o_ref, u_s, a_s, b_s):
    row = _iota((L, 1), 0)
    u_s[...] = _seg_conv4(in_ref[:, 0:LRU_W], cw_ref[...], row) + cb_ref[...]

    sub = _iota((LRU_RB, 1), 0) % 8
    n_blk = L // LRU_RB
    n_ctx_blk = CTX // LRU_RB

    for d in range(2):
        sp = _softplus(-lam_ref[d])

        def coeffs(t, carry, d=d, sp=sp):
            rows = pl.ds(pl.multiple_of(t * TM, TM), TM)
            ub = u_s[rows, :]
            pre = _dot(ub.astype(BF16), wg_ref[d]) + gb_ref[d]
            r = _sigmoid(pre[:, 0:LRU_W])
            i = _sigmoid(pre[:, LRU_W:])
            log_a = -LRU_C * r * sp
            a = jnp.exp(log_a)
            a_s[rows, :] = a
            b_s[rows, :] = jnp.sqrt(jnp.tanh(-log_a) * (a * a + 1.0)) * (i * ub)
            return carry

        lax.fori_loop(0, N_TILES, coeffs, 0)

        def scan_block(t, hin, d=d):
            if d == 0:
                blk = t
            else:
                blk = jnp.where(t < n_ctx_blk, n_ctx_blk - 1 - t, n_blk + n_ctx_blk - 1 - t)
            rows = pl.ds(pl.multiple_of(blk * LRU_RB, LRU_RB), LRU_RB)
            a = a_s[rows, :]
            b = b_s[rows, :]
            for s in (1, 2, 4):
                if d == 0:
                    valid = sub >= s
                    shift = s
                else:
                    valid = sub < 8 - s
                    shift = LRU_RB - s
                a_sh = jnp.where(valid, pltpu.roll(a, shift, 0), 1.0)
                b_sh = jnp.where(valid, pltpu.roll(b, shift, 0), 0.0)
                b = a * b_sh + b
                a = a * a_sh
            tiles = range(LRU_RB // 8) if d == 0 else range(LRU_RB // 8 - 1, -1, -1)
            hs = [None] * (LRU_RB // 8)
            for ti in tiles:
                ht = a[ti * 8:(ti + 1) * 8, :] * hin + b[ti * 8:(ti + 1) * 8, :]
                hs[ti] = ht
                hin = ht[7:8, :] if d == 0 else ht[0:1, :]
            hblk = jnp.concatenate(hs, axis=0)
            if d == 0:
                o_ref[rows, :] = hblk
            else:
                o_ref[rows, :] += hblk
            return hin

        lax.fori_loop(0, n_blk, scan_block, jnp.zeros((1, LRU_W), F32))

    y = in_ref[:, LRU_W:]
    gelu = 0.5 * y * (1.0 + jnp.tanh(np.sqrt(2.0 / np.pi) * (y + 0.044715 * (y * y * y))))
    o_ref[...] = o_ref[...] * gelu


def _lru_call(lru, conv_w, conv_b, wg, gb, lam):
    bsz = lru.shape[0]

    def small(shape):
        nd = len(shape)
        return pl.BlockSpec(shape, lambda b: (0,) * nd)

    return pl.pallas_call(
        _lru_kernel,
        grid=(bsz,),
        in_specs=[pl.BlockSpec((None, L, 2 * LRU_W), lambda b: (b, 0, 0)),
                  small((4, LRU_W)), small((1, LRU_W)), small((2, LRU_W, 2 * LRU_W)),
                  small((2, 1, 2 * LRU_W)), small((2, 1, LRU_W))],
        out_specs=pl.BlockSpec((None, L, LRU_W), lambda b: (b, 0, 0)),
        out_shape=jax.ShapeDtypeStruct((bsz, L, LRU_W), F32),
        scratch_shapes=[pltpu.VMEM((L, LRU_W), F32)] * 3,
        compiler_params=_cparams(("arbitrary",), 56),
        name="lru",
    )(lru, conv_w, conv_b, wg, gb, lam)


def _merge_kernel(a_ref, b_ref, c_ref, gl_ref, x_ref, mod_ref, g2_ref, wb_ref, wo_ref, wr_ref,
                  xn_ref, h2_ref, aff_ref):
    mixed = None
    for i, br in enumerate((a_ref, b_ref, c_ref)):
        term = _sigmoid(gl_ref[:, i * D:(i + 1) * D]) * _dot(br[...].astype(BF16), wb_ref[i])
        mixed = term if mixed is None else mixed + term
    mod = mod_ref[...]
    xn = x_ref[...] + mod[2:3] * _dot(mixed.astype(BF16), wo_ref[...])
    xn_ref[...] = xn
    h2 = _modulated_norm(xn, g2_ref[...], mod[3:4], mod[4:5])
    h2_ref[...] = h2.astype(BF16)
    logits = _dot3(h2, wr_ref[...])
    e = jnp.exp(logits - jnp.max(logits, axis=-1, keepdims=True))
    aff_ref[...] = e / jnp.sum(e, axis=-1, keepdims=True)


def _merge_call(a_br, b_br, c_br, gate, xt, modsel, g2, wb, wo, wr):
    bsz = xt.shape[0]
    return pl.pallas_call(
        _merge_kernel,
        grid=(bsz, N_TILES),
        in_specs=[_tile_spec(512), _tile_spec(512), _tile_spec(512), _tile_spec(3 * D), _tile_spec(D),
                  _mod_spec(), _const_spec((1, D)), _const_spec((3, 512, D)), _const_spec((D, D)),
                  _const_spec((D, N_EXP))],
        out_specs=[_tile_spec(D), _tile_spec(D), _tile_spec(N_EXP)],
        out_shape=[jax.ShapeDtypeStruct((bsz, L, D), F32), jax.ShapeDtypeStruct((bsz, L, D), BF16),
                   jax.ShapeDtypeStruct((bsz, L, N_EXP), F32)],
        compiler_params=_cparams(("arbitrary", "arbitrary"), 40),
        name="merge",
    )(a_br, b_br, c_br, gate, xt, modsel, g2, wb, wo, wr)


def _excl_prefix(m):
    w = 256
    pp = _iota((w, w), 0)
    jj = _iota((w, w), 1)
    upper = jnp.where(pp < jj, 1.0, 0.0).astype(BF16)
    off = jnp.zeros((m.shape[0], 1), F32)
    parts = []
    for c0 in range(0, m.shape[1], w):
        blk = m[:, c0:c0 + w]
        parts.append(_dot(blk.astype(BF16), upper) + off)
        off = off + jnp.sum(blk, axis=1, keepdims=True)
    return parts[0] if len(parts) == 1 else jnp.concatenate(parts, axis=1)


def _topk_pos(v, k):
    bits = pltpu.bitcast(v, I32)
    thr = jnp.zeros((v.shape[0], 1), I32)
    for bit in range(30, -1, -1):
        cand = thr | (1 << bit)
        cnt = jnp.sum(jnp.where(bits >= cand, 1.0, 0.0), axis=1, keepdims=True)
        thr = jnp.where(cnt >= k, cand, thr)
    gt = bits > thr
    eq = jnp.where(bits == thr, 1.0, 0.0)
    need = k - jnp.sum(jnp.where(gt, 1.0, 0.0), axis=1, keepdims=True)
    sel = jnp.where(gt, 1.0, jnp.where(_excl_prefix(eq) < need, eq, 0.0))
    return jnp.where(sel > 0.0, _excl_prefix(sel), -1.0).astype(I32)


def _select_kernel(aff_ref, pos_ref):
    pos_ref[:, 0:CTX] = _topk_pos(aff_ref[:, 0:CTX], CAP_C)
    pos_ref[:, CTX:L] = _topk_pos(aff_ref[:, CTX:L], CAP_X)


def _select_call(aff_t):
    bsz = aff_t.shape[0]
    return pl.pallas_call(
        _select_kernel,
        grid=(bsz,),
        in_specs=[pl.BlockSpec((None, N_EXP, L), lambda b: (b, 0, 0))],
        out_specs=pl.BlockSpec((None, N_EXP, L), lambda b: (b, 0, 0)),
        out_shape=jax.ShapeDtypeStruct((bsz, N_EXP, L), I32),
        compiler_params=_cparams(("arbitrary",), 32),
        name="select",
    )(aff_t)


GATHER_EXPERTS = 4


def _gather_kernel(pos_ref, aff_ref, h_ref, xs_ref, xc_ref, gx_ref, gc_ref):
    for lo, hi, cap, x_ref, g_ref in ((CTX, L, CAP_X, xs_ref, gx_ref), (0, CTX, CAP_C, xc_ref, gc_ref)):
        ranks = _iota((cap, hi - lo), 0)
        hits = [ranks == pos_ref[g][:, lo:hi] for g in range(GATHER_EXPERTS)]
        onehot = jnp.concatenate([jnp.where(hit, 1.0, 0.0).astype(BF16) for hit in hits], axis=0)
        rows = _dot(onehot, h_ref[lo:hi, :]).astype(BF16)
        for g in range(GATHER_EXPERTS):
            x_ref[g] = rows[g * cap:(g + 1) * cap, :]
            g_ref[g] = jnp.sum(jnp.where(hits[g], aff_ref[g][:, lo:hi], 0.0), axis=1, keepdims=True)


def _gather_call(pos4, aff4, h2):
    bsz = h2.shape[0]
    row_spec = pl.BlockSpec((None, GATHER_EXPERTS, 1, L), lambda b, e: (b, e, 0, 0))

    def cap_spec(cap, width):
        return pl.BlockSpec((GATHER_EXPERTS, cap, width), lambda b, e: (e, b, 0))

    return pl.pallas_call(
        _gather_kernel,
        grid=(bsz, N_EXP // GATHER_EXPERTS),
        in_specs=[row_spec, row_spec, pl.BlockSpec((None, L, D), lambda b, e: (b, 0, 0))],
        out_specs=[cap_spec(CAP_X, D), cap_spec(CAP_C, D), cap_spec(CAP_X, 1), cap_spec(CAP_C, 1)],
        out_shape=[jax.ShapeDtypeStruct((N_EXP, bsz * CAP_X, D), BF16),
                   jax.ShapeDtypeStruct((N_EXP, bsz * CAP_C, D), BF16),
                   jax.ShapeDtypeStruct((N_EXP, bsz * CAP_X, 1), F32),
                   jax.ShapeDtypeStruct((N_EXP, bsz * CAP_C, 1), F32)],
        compiler_params=_cparams(("arbitrary", "arbitrary"), 48),
        name="gather",
    )(pos4, aff4, h2)


FFN_HC = 512
FFN_MT = 512


def _ffn_kernel(xs_ref, xc_ref, gx_ref, gc_ref, wg_ref, wu_ref, wd_ref, ys_ref, yc_ref,
                acc_x, acc_c, wgb, wub, wdb):
    hc = pl.program_id(1)
    wgb[...] = wg_ref[...].astype(BF16)
    wub[...] = wu_ref[...].astype(BF16)
    wdb[...] = wd_ref[...].astype(BF16)

    def ffn(x):
        g = _dot(x, wgb[...])
        hid = _silu(g) * _dot(x, wub[...])
        return _dot(hid.astype(BF16), wdb[...])

    @pl.when(hc == 0)
    def _():
        acc_x[...] = jnp.zeros_like(acc_x)
        acc_c[...] = jnp.zeros_like(acc_c)

    for m0 in range(0, xs_ref.shape[0], FFN_MT):
        acc_x[m0:m0 + FFN_MT, :] += ffn(xs_ref[m0:m0 + FFN_MT, :])
    acc_c[...] += ffn(xc_ref[...])

    @pl.when(hc == pl.num_programs(1) - 1)
    def _():
        ys_ref[...] = (acc_x[...] * gx_ref[...]).astype(BF16)
        yc_ref[...] = (acc_c[...] * gc_ref[...]).astype(BF16)


def _ffn_call(xs, xc, gx, gc, w_gate, w_up, w_down, layer):
    mx, mc = xs.shape[1], xc.shape[1]
    return pl.pallas_call(
        _ffn_kernel,
        grid=(N_EXP, EXP_HID // FFN_HC),
        in_specs=[pl.BlockSpec((None, mx, D), lambda e, j: (e, 0, 0)),
                  pl.BlockSpec((None, mc, D), lambda e, j: (e, 0, 0)),
                  pl.BlockSpec((None, mx, 1), lambda e, j: (e, 0, 0)),
                  pl.BlockSpec((None, mc, 1), lambda e, j: (e, 0, 0)),
                  pl.BlockSpec((None, None, D, FFN_HC), lambda e, j: (layer, e, 0, j)),
                  pl.BlockSpec((None, None, D, FFN_HC), lambda e, j: (layer, e, 0, j)),
                  pl.BlockSpec((None, None, FFN_HC, D), lambda e, j: (layer, e, j, 0))],
        out_specs=[pl.BlockSpec((None, mx, D), lambda e, j: (e, 0, 0)),
                   pl.BlockSpec((None, mc, D), lambda e, j: (e, 0, 0))],
        out_shape=[jax.ShapeDtypeStruct(xs.shape, BF16), jax.ShapeDtypeStruct(xc.shape, BF16)],
        scratch_shapes=[pltpu.VMEM((mx, D), F32), pltpu.VMEM((mc, D), F32),
                        pltpu.VMEM((D, FFN_HC), BF16), pltpu.VMEM((D, FFN_HC), BF16),
                        pltpu.VMEM((FFN_HC, D), BF16)],
        compiler_params=_cparams(("arbitrary", "arbitrary"), 56),
        name="ffn",
    )(xs, xc, gx, gc, w_gate, w_up, w_down)


SCATTER_TM = 768


def _scatter_kernel(ys_ref, yc_ref, pos_ref, x_ref, g2_ref, o_ref):
    t = pl.program_id(1)

    def scattered(r0, r1, y_ref, cap):
        pos = pos_ref[r0:r1, :]
        ranks = _iota((r1 - r0, cap), 1)
        onehot = jnp.concatenate(
            [jnp.where(pos[:, e:e + 1] == ranks, 1.0, 0.0).astype(BF16) for e in range(N_EXP)], axis=1)
        return _dot(onehot, y_ref[...].reshape(N_EXP * cap, D))

    @pl.when(t == 0)
    def _():
        o_ref[0:CTX, :] = x_ref[0:CTX, :] + g2_ref[0] * scattered(0, CTX, yc_ref, CAP_C)
        o_ref[CTX:, :] = x_ref[CTX:, :] + g2_ref[1] * scattered(CTX, SCATTER_TM, ys_ref, CAP_X)

    @pl.when(t > 0)
    def _():
        o_ref[...] = x_ref[...] + g2_ref[1] * scattered(0, SCATTER_TM, ys_ref, CAP_X)


def _scatter_call(ys, yc, pos_t, xn, gate2):
    bsz = xn.shape[0]

    def rows_spec(width):
        return pl.BlockSpec((None, SCATTER_TM, width), lambda b, t: (b, t, 0))

    return pl.pallas_call(
        _scatter_kernel,
        grid=(bsz, L // SCATTER_TM),
        in_specs=[pl.BlockSpec((N_EXP, CAP_X, D), lambda b, t: (0, b, 0)),
                  pl.BlockSpec((N_EXP, CAP_C, D), lambda b, t: (0, b, 0)),
                  rows_spec(N_EXP), rows_spec(D),
                  pl.BlockSpec((None, 2, 1, D), lambda b, t: (b, 0, 0, 0))],
        out_specs=rows_spec(D),
        out_shape=jax.ShapeDtypeStruct((bsz, L, D), F32),
        compiler_params=_cparams(("arbitrary", "arbitrary"), 56),
        name="scatter",
    )(ys, yc, pos_t, xn, gate2)


def _rope_tables():
    rows = SEQ // GRID_W
    row = np.repeat(np.arange(rows, dtype=np.float32), GRID_W)
    col = np.tile(np.arange(GRID_W, dtype=np.float32), rows)
    axis_dim = SWA_DH // 2
    inv_freq = (np.float32(ROPE_THETA) ** (-np.arange(0, axis_dim, 2, dtype=np.float32) / np.float32(axis_dim)))
    inv_freq = inv_freq.astype(np.float32)
    ang = np.concatenate([row[:, None] * inv_freq, col[:, None] * inv_freq], axis=-1).astype(np.float32)
    ang = np.tile(np.repeat(ang, 2, axis=-1), (1, 128 // SWA_DH))
    return jnp.asarray(np.cos(ang), F32), jnp.asarray(np.sin(ang), F32)


def _block_diag(w):
    n, bw, _ = w.shape
    eye = jnp.eye(n, dtype=w.dtype)
    return (eye[:, None, :, None] * w[:, :, None, :]).reshape(n * bw, n * bw)


def _pad16(v):
    return jnp.concatenate([v.reshape(-1), jnp.zeros((16 - v.size,), v.dtype)])


def _layer(xt, mods_l, layer, norm1_g, norm2_g, w_in, gdn_conv_w, gdn_a_log, gdn_dt_bias, gdn_onorm_g,
           swa_qnorm_g, swa_knorm_g, swa_sink, lru_conv_w, lru_conv_b, lru_gate_w, lru_gate_b, lru_lambda,
           w_branch, w_out, w_router, w_exp_gate, w_exp_up, w_exp_down, cos_t, sin_t):
    bsz = xt.shape[0]
    mods6 = mods_l.reshape(16, 6, D)
    modsel = jnp.stack([jnp.broadcast_to(mods6[8], (bsz, 6, D)), mods6[:bsz]], axis=1)

    w_main = jnp.concatenate([w_in[:, :2048], w_in[:, 2064:]], axis=1).astype(BF16)
    w_ab = w_in[:, 2048:2064].astype(BF16)
    gdn, swa, lru, gate, ab = _inproj_call(xt, modsel, norm1_g.reshape(1, D), w_main, w_ab)

    abt = jnp.swapaxes(ab.reshape(bsz, N_CHUNK, GDN_C, 16), 2, 3)
    alog16, dt16 = _pad16(gdn_a_log), _pad16(gdn_dt_bias)
    a_br = _gdn_call(gdn, ab, abt, gdn_conv_w, alog16.reshape(1, 16), dt16.reshape(1, 16),
                     alog16.reshape(16, 1), dt16.reshape(16, 1), gdn_onorm_g.reshape(1, GDN_DK))

    b_br = _swa_call(swa, jnp.tile(swa_qnorm_g, 128 // SWA_DH).reshape(1, 128),
                     jnp.tile(swa_knorm_g, 128 // SWA_DH).reshape(1, 128), swa_sink, cos_t, sin_t)

    wg = jnp.stack([jnp.concatenate([_block_diag(lru_gate_w[d, 0]), _block_diag(lru_gate_w[d, 1])], axis=1)
                    for d in range(2)]).astype(BF16)
    c_br = _lru_call(lru, lru_conv_w, lru_conv_b.reshape(1, LRU_W), wg,
                     lru_gate_b.reshape(2, 1, 2 * LRU_W), lru_lambda.reshape(2, 1, LRU_W))

    xn, h2, aff = _merge_call(a_br, b_br, c_br, gate, xt, modsel, norm2_g.reshape(1, D),
                              w_branch.astype(BF16), w_out.astype(BF16), w_router)

    aff_t = jnp.swapaxes(aff, 1, 2)
    pos = _select_call(aff_t)
    xs, xc, gx, gc = _gather_call(pos.reshape(bsz, N_EXP, 1, L), aff_t.reshape(bsz, N_EXP, 1, L), h2)
    ys, yc = _ffn_call(xs, xc, gx, gc, w_exp_gate, w_exp_up, w_exp_down, layer)
    gate2 = modsel[:, :, 5:6, :]
    return _scatter_call(ys, yc, jnp.swapaxes(pos, 1, 2), xn, gate2)


def kernel(x, c, ctx, c_ctx, w_ada, b_ada, norm1_g, norm2_g, w_in, gdn_conv_w, gdn_a_log, gdn_dt_bias, gdn_onorm_g, swa_qnorm_g, swa_knorm_g, swa_sink, lru_conv_w, lru_conv_b, lru_gate_w, lru_gate_b, lru_lambda, w_branch, w_out, w_router, w_exp_gate, w_exp_up, w_exp_down):
    bsz = x.shape[0]
    depth = w_ada.shape[0]
    assert x.shape == (bsz, SEQ, D) and ctx.shape == (bsz, CTX, D) and bsz <= 8
    cond16 = jnp.concatenate([c, jnp.zeros((8 - bsz, D), F32), c_ctx[None, :], jnp.zeros((7, D), F32)], axis=0)
    mods = _ada_call(cond16, w_ada, b_ada)
    cos_t, sin_t = _rope_tables()
    xt = jnp.concatenate([ctx, x], axis=1)
    for l in range(depth):
        xt = _layer(xt, mods[l], l, norm1_g[l], norm2_g[l], w_in[l], gdn_conv_w[l], gdn_a_log[l],
                    gdn_dt_bias[l], gdn_onorm_g[l], swa_qnorm_g[l], swa_knorm_g[l], swa_sink[l],
                    lru_conv_w[l], lru_conv_b[l], lru_gate_w[l], lru_gate_b[l], lru_lambda[l],
                    w_branch[l], w_out[l], w_router[l], w_exp_gate, w_exp_up, w_exp_down, cos_t, sin_t)
    return xt[:, CTX:, :]
```

```python
import functools

import jax
import jax.numpy as jnp
import numpy as np
from jax import lax
from jax.experimental import pallas as pl
from jax.experimental.pallas import tpu as pltpu

F32 = jnp.float32
BF16 = jnp.bfloat16
I32 = jnp.int32

D = 1024
SEQ = 2048
CTX = 256
L = CTX + SEQ
GRID_W = 64
EPS = 1e-6
NEG_INF = -1e30
ROPE_THETA = 10000.0

GDN_H = 4
GDN_DK = 128
GDN_C = 64
N_CHUNK = L // GDN_C
N_CTX_CHUNK = CTX // GDN_C

SWA_QH = 8
SWA_KVH = 2
SWA_GROUP = SWA_QH // SWA_KVH
SWA_DH = 64
SWA_BLK = 128

LRU_W = 512
LRU_NBLK = 8
LRU_C = 8.0

N_EXP = 16
EXP_HID = 2048
CAP_X = 2 * SEQ // N_EXP
CAP_C = 2 * CTX // N_EXP

TM = 256
N_TILES = L // TM
MAIN_SPLITS = (2048, 768, 1024, 3072)
N_MAIN = sum(MAIN_SPLITS)

VMEM_MIB_V7X = 64


def _cparams(sem, vmem_mib):
    assert vmem_mib < VMEM_MIB_V7X
    return pltpu.CompilerParams(dimension_semantics=sem, vmem_limit_bytes=vmem_mib * 1024 * 1024)


def _dot(a, b):
    return jnp.dot(a, b, preferred_element_type=F32)


def _dot_nt(a, b):
    return lax.dot_general(a, b, (((1,), (1,)), ((), ())), preferred_element_type=F32)


def _dot_tn(a, b):
    return lax.dot_general(a, b, (((0,), (0,)), ((), ())), preferred_element_type=F32)


def _split(x):
    hi = x.astype(BF16)
    lo = (x - hi.astype(F32)).astype(BF16)
    return hi, lo


def _dot3(a, b):
    ah, al = _split(a)
    bh, bl = _split(b)
    return _dot(ah, bh) + _dot(al, bh) + _dot(ah, bl)


def _sigmoid(x):
    return 0.5 * jnp.tanh(0.5 * x) + 0.5


def _silu(x):
    return x * _sigmoid(x)


def _softplus(x):
    return jnp.maximum(x, 0.0) + jnp.log1p(jnp.exp(-jnp.abs(x)))


def _iota(shape, dim):
    return lax.broadcasted_iota(I32, shape, dim)


def _seg_conv4(x, w, row):
    n = x.shape[0]
    segpos = jnp.where(row < CTX, row, row - CTX)
    is_last = (row == CTX - 1) | (row == n - 1)
    xm2 = jnp.where(segpos >= 2, pltpu.roll(x, 2, 0), 0.0)
    xm1 = jnp.where(segpos >= 1, pltpu.roll(x, 1, 0), 0.0)
    xp1 = jnp.where(is_last, 0.0, pltpu.roll(x, n - 1, 0))
    return xm2 * w[0:1] + xm1 * w[1:2] + x * w[2:3] + xp1 * w[3:4]


def _ada_kernel(c_ref, w_ref, b_ref, o_ref):
    o_ref[...] = _dot3(_silu(c_ref[...]), w_ref[...]) + b_ref[...]


def _ada_call(cond16, w_ada, b_ada):
    depth, _, n = w_ada.shape
    tn = 1536
    return pl.pallas_call(
        _ada_kernel,
        grid=(depth, n // tn),
        in_specs=[pl.BlockSpec((16, D), lambda l, j: (0, 0)),
                  pl.BlockSpec((None, D, tn), lambda l, j: (l, 0, j)),
                  pl.BlockSpec((None, 1, tn), lambda l, j: (l, 0, j))],
        out_specs=pl.BlockSpec((None, 16, tn), lambda l, j: (l, 0, j)),
        out_shape=jax.ShapeDtypeStruct((depth, 16, n), F32),
        compiler_params=_cparams(("arbitrary", "arbitrary"), 32),
        name="ada",
    )(cond16, w_ada, b_ada.reshape(depth, 1, n))


def _modulated_norm(x, g, shift, scale):
    xn = x * lax.rsqrt(jnp.mean(x * x, axis=-1, keepdims=True) + EPS) * g
    return xn * (1.0 + scale) + shift


def _inproj_kernel(x_ref, mod_ref, g_ref, wm_ref, wab_ref, gdn_ref, swa_ref, lru_ref, gate_ref, ab_ref):
    mod = mod_ref[...]
    hb = _modulated_norm(x_ref[...], g_ref[...], mod[0:1], mod[1:2]).astype(BF16)
    col = 0
    for ref, width in zip((gdn_ref, swa_ref, lru_ref, gate_ref), MAIN_SPLITS):
        for c0 in range(0, width, 512):
            cw = min(512, width - c0)
            ref[:, c0:c0 + cw] = _dot(hb, wm_ref[:, col + c0:col + c0 + cw])
        col += width
    ab_ref[...] = _dot(hb, wab_ref[...])


def _mod_spec():
    return pl.BlockSpec((None, None, 6, D), lambda b, t: (b, jnp.minimum(t, 1), 0, 0))


def _tile_spec(width):
    return pl.BlockSpec((None, TM, width), lambda b, t: (b, t, 0))


def _const_spec(shape):
    nd = len(shape)
    return pl.BlockSpec(shape, lambda b, t: (0,) * nd, pipeline_mode=pl.Buffered(1))


def _layer_spec(shape, layer):
    nd = len(shape)
    return pl.BlockSpec((None,) + shape, lambda b, t: (layer,) + (0,) * nd, pipeline_mode=pl.Buffered(1))


def _inproj_call(xt, modsel, g1, w_main, w_ab, layer):
    bsz = xt.shape[0]
    outs = [jax.ShapeDtypeStruct((bsz, L, w), F32) for w in MAIN_SPLITS] + [jax.ShapeDtypeStruct((bsz, L, 16), F32)]
    return pl.pallas_call(
        _inproj_kernel,
        grid=(bsz, N_TILES),
        in_specs=[_tile_spec(D), _mod_spec(), _const_spec((1, D)),
                  _layer_spec((D, N_MAIN), layer), _layer_spec((D, 16), layer)],
        out_specs=[_tile_spec(w) for w in MAIN_SPLITS] + [_tile_spec(16)],
        out_shape=outs,
        compiler_params=_cparams(("arbitrary", "arbitrary"), 48),
        name="inproj",
    )(xt, modsel, g1, w_main, w_ab)


GDN_HPS = 2
GDN_PREP_CHUNKS = 6


def _gdn_kernel(q_ref, k_ref, v_ref, z_ref, cwq_ref, cwk_ref, cwv_ref, ab_ref, abt_ref,
                alog_r_ref, dt_r_ref, alog_c_ref, dt_c_ref, og_ref, o_ref,
                qs, ks, vs, kq_s, n_s, eg_s, orev_s):
    head0 = pl.program_id(1) * GDN_HPS
    row = _iota((L, 1), 0)

    for hh in range(GDN_HPS):
        hc = slice(hh * GDN_DK, (hh + 1) * GDN_DK)
        q = _silu(_seg_conv4(q_ref[:, hc], cwq_ref[:, hc], row))
        qs[hh] = q * lax.rsqrt(jnp.sum(q * q, axis=-1, keepdims=True) + EPS) * (GDN_DK ** -0.5)
        k = _silu(_seg_conv4(k_ref[:, hc], cwk_ref[:, hc], row))
        ks[hh] = k * lax.rsqrt(jnp.sum(k * k, axis=-1, keepdims=True) + EPS)
        vs[hh] = _silu(_seg_conv4(v_ref[:, hc], cwv_ref[:, hc], row))

    c = GDN_C
    ii = _iota((c, c), 0)
    jj = _iota((c, c), 1)
    eye = jnp.where(ii == jj, 1.0, 0.0)
    blk = [(ii // b) == (jj // b) for b in (8, 16, 32, 64)]
    lane16 = _iota((c, 16), 1)
    sub16 = _iota((16, c), 0)

    def obuf(hh, d):
        return o_ref.at[:, hh * GDN_DK:(hh + 1) * GDN_DK] if d == 0 else orev_s.at[hh]

    def prep(it, carry):
        chunks = []
        for u in range(GDN_PREP_CHUNKS):
            ci = it * GDN_PREP_CHUNKS + u
            rows = pl.ds(pl.multiple_of(ci * c, c), c)
            ab = ab_ref[rows, :]
            g_all = -jnp.exp(alog_r_ref[...]) * _softplus(ab + dt_r_ref[...])
            beta_all = _sigmoid(ab)
            g_all_t = -jnp.exp(alog_c_ref[...]) * _softplus(abt_ref[ci] + dt_c_ref[...])
            for hh in range(GDN_HPS):
                qc, kc, vc = qs[hh, rows, :], ks[hh, rows, :], vs[hh, rows, :]
                chunks.append(dict(ci=ci, rows=rows, hh=hh, qc=qc, kc=kc, vc=vc, g_all=g_all, beta_all=beta_all,
                                   g_all_t=g_all_t, kcb=kc.astype(BF16), qcb=qc.astype(BF16)))
        for ch in chunks:
            ch["kk"] = _dot_nt(ch["kcb"], ch["kcb"])
        for ch in chunks:
            ch["qk"] = _dot_nt(ch["qcb"], ch["kcb"])
        chains = []
        for ch in chunks:
            g_all, beta_all, g_all_t = ch["g_all"], ch["beta_all"], ch["g_all_t"]
            for d in range(2):
                if d == 0:
                    incl, strict, incl_t = jj <= ii, jj < ii, ii <= jj
                else:
                    incl, strict, incl_t = jj >= ii, jj > ii, ii >= jj
                a_idx = d * GDN_H + head0 + ch["hh"]
                gcol = jnp.sum(jnp.where(lane16 == a_idx, g_all, 0.0), axis=1, keepdims=True)
                bcol = jnp.sum(jnp.where(lane16 == 2 * GDN_H + a_idx, beta_all, 0.0), axis=1, keepdims=True)
                grow = jnp.sum(jnp.where(sub16 == a_idx, g_all_t, 0.0), axis=0, keepdims=True)
                gbc = jnp.sum(jnp.where(incl, grow, 0.0), axis=1, keepdims=True)
                gbr = jnp.sum(jnp.where(incl_t, gcol, 0.0), axis=0, keepdims=True)
                decay = jnp.where(incl, jnp.exp(jnp.where(incl, gbc - gbr, 0.0)), 0.0)
                a_mat = jnp.where(strict, ch["kk"] * bcol * decay, 0.0)
                egbc = jnp.exp(gbc)
                glast = gbc[c - 1:c, :] if d == 0 else gbc[0:1, :]
                chains.append(dict(
                    ch=ch, d=d, a=a_mat, egbc=egbc, eglast=jnp.exp(glast),
                    rhs=jnp.concatenate([ch["vc"] * bcol, ch["kc"] * (bcol * egbc)], axis=1).astype(BF16),
                    kd=(ch["kc"] * jnp.exp(glast - gbc)).astype(BF16),
                    qkd=(ch["qk"] * decay).astype(BF16)))
        for cn in chains:
            a8 = jnp.where(blk[0], cn["a"], 0.0)
            a8b = a8.astype(BF16)
            cn["p2"] = _dot(a8b, a8b)
            cn["ima"] = (eye - a8).astype(BF16)
        for cn in chains:
            p2b = cn["p2"].astype(BF16)
            cn["p4"] = _dot(p2b, p2b)
            cn["inv"] = _dot(cn["ima"], (eye + cn["p2"]).astype(BF16))
        for cn in chains:
            cn["inv"] = _dot(cn["inv"].astype(BF16), (eye + cn["p4"]).astype(BF16))
        for lvl in range(1, len(blk)):
            for cn in chains:
                off = jnp.where(blk[lvl] & jnp.logical_not(blk[lvl - 1]), cn["a"], 0.0)
                cn["ot"] = _dot(off.astype(BF16), cn["inv"].astype(BF16))
            for cn in chains:
                cn["inv"] = cn["inv"] - _dot(cn["inv"].astype(BF16), cn["ot"].astype(BF16))
        for cn in chains:
            cn["uw"] = _dot(cn["inv"].astype(BF16), cn["rhs"]).astype(BF16)
        for cn in chains:
            cn["kn"] = _dot_tn(cn["kd"], cn["uw"])
        for cn in chains:
            cn["qo"] = _dot(cn["qkd"], cn["uw"])
        for cn in chains:
            d, ci, rows, hh = cn["d"], cn["ch"]["ci"], cn["ch"]["rows"], cn["ch"]["hh"]
            kq_s[hh, d, ci, 0:GDN_DK, :] = cn["kn"][:, GDN_DK:].astype(BF16)
            kq_s[hh, d, ci, GDN_DK:, :] = (cn["ch"]["qc"] * cn["egbc"] - cn["qo"][:, GDN_DK:]).astype(BF16)
            n_s[hh, d, ci] = cn["kn"][:, 0:GDN_DK]
            obuf(hh, d)[rows, :] = cn["qo"][:, 0:GDN_DK]
            eg_s[hh, d, ci] = jnp.broadcast_to(cn["eglast"], (8, GDN_DK))
        return carry

    lax.fori_loop(0, N_CHUNK // GDN_PREP_CHUNKS, prep, 0)

    def step(t, states):
        ci_rev = jnp.where(t < N_CTX_CHUNK, N_CTX_CHUNK - 1 - t, N_CHUNK + N_CTX_CHUNK - 1 - t)
        cis = (t, ci_rev)
        chains = [(hh, d) for hh in range(GDN_HPS) for d in range(2)]
        rs = [_dot(kq_s[hh, d, cis[d]], states[i].astype(BF16)) for i, (hh, d) in enumerate(chains)]
        new_states = []
        for i, (hh, d) in enumerate(chains):
            rows = pl.ds(pl.multiple_of(cis[d] * c, c), c)
            obuf(hh, d)[rows, :] += rs[i][GDN_DK:, :]
            new_states.append(states[i] * eg_s[hh, d, cis[d]][0:1, :] + n_s[hh, d, cis[d]] - rs[i][0:GDN_DK, :])
        return tuple(new_states)

    zero_state = jnp.zeros((GDN_DK, GDN_DK), F32)
    lax.fori_loop(0, N_CHUNK, step, (zero_state,) * (2 * GDN_HPS))

    for hh in range(GDN_HPS):
        hc = slice(hh * GDN_DK, (hh + 1) * GDN_DK)
        o = o_ref[:, hc] + orev_s[hh]
        y = o * lax.rsqrt(jnp.mean(o * o, axis=-1, keepdims=True) + EPS) * og_ref[...]
        o_ref[:, hc] = y * _silu(z_ref[:, hc])


def _gdn_call(gdn, ab, abt, conv_w, alog_r, dt_r, alog_c, dt_c, onorm_g):
    bsz = gdn.shape[0]

    wblk = GDN_HPS * GDN_DK
    nblk = GDN_H // GDN_HPS

    def head_spec(part):
        return pl.BlockSpec((None, L, wblk), lambda b, h: (b, 0, part * nblk + h))

    def cw_spec(part):
        return pl.BlockSpec((4, wblk), lambda b, h: (0, part * nblk + h))

    def small(shape):
        nd = len(shape)
        return pl.BlockSpec(shape, lambda b, h: (0,) * nd)

    seq = pltpu.VMEM((GDN_HPS, L, GDN_DK), F32)
    return pl.pallas_call(
        _gdn_kernel,
        grid=(bsz, nblk),
        in_specs=[head_spec(0), head_spec(1), head_spec(2), head_spec(3),
                  cw_spec(0), cw_spec(1), cw_spec(2),
                  pl.BlockSpec((None, L, 16), lambda b, h: (b, 0, 0)),
                  pl.BlockSpec((None, N_CHUNK, 16, GDN_C), lambda b, h: (b, 0, 0, 0)),
                  small((1, 16)), small((1, 16)), small((16, 1)), small((16, 1)), small((1, GDN_DK))],
        out_specs=pl.BlockSpec((None, L, wblk), lambda b, h: (b, 0, h)),
        out_shape=jax.ShapeDtypeStruct((bsz, L, GDN_H * GDN_DK), F32),
        scratch_shapes=[seq, seq, seq,
                        pltpu.VMEM((GDN_HPS, 2, N_CHUNK, GDN_DK + GDN_C, GDN_DK), BF16),
                        pltpu.VMEM((GDN_HPS, 2, N_CHUNK, GDN_DK, GDN_DK), F32),
                        pltpu.VMEM((GDN_HPS, 2, N_CHUNK, 8, GDN_DK), F32), seq],
        compiler_params=_cparams(("arbitrary", "arbitrary"), 60),
        name="gdn",
    )(gdn, gdn, gdn, gdn, conv_w, conv_w, conv_w, ab, abt, alog_r, dt_r, alog_c, dt_c, onorm_g)


def _swa_kernel(sink_ref, in_ref, qg_ref, kg_ref, cos_ref, sin_ref, o_ref, q_s, kp_s, vp_s, kc_s, vc_s):
    ii = _iota((128, 128), 0)
    jj = _iota((128, 128), 1)
    head_ones = jnp.where((ii // SWA_DH) == (jj // SWA_DH), 1.0, 0.0).astype(BF16)
    even = (_iota((1, 128), 1) % 2) == 0

    def norm(t, g):
        ssq = _dot((t * t).astype(BF16), head_ones)
        return t * lax.rsqrt(ssq * (1.0 / SWA_DH) + EPS) * g

    def rope(t):
        rot = jnp.where(even, -pltpu.roll(t, 127, 1), pltpu.roll(t, 1, 1))
        return t * cos_ref[...] + rot * sin_ref[...]

    scale = SWA_DH ** -0.5
    for cg in range(SWA_QH * SWA_DH // 128):
        cols = slice(cg * 128, (cg + 1) * 128)
        q_s[0:CTX, cols] = (norm(in_ref[0:CTX, cols], qg_ref[...]) * scale).astype(BF16)
        q_s[CTX:L, cols] = (rope(norm(in_ref[CTX:L, cols], qg_ref[...])) * scale).astype(BF16)
    kcols = slice(512, 640)
    vcols = slice(640, 768)
    zeros_blk = jnp.zeros((SWA_BLK, 128), BF16)
    kc_s[...] = norm(in_ref[0:CTX, kcols], kg_ref[...]).astype(BF16)
    kp_s[0:SWA_BLK, :] = zeros_blk
    kp_s[SWA_BLK:SWA_BLK + SEQ, :] = rope(norm(in_ref[CTX:L, kcols], kg_ref[...])).astype(BF16)
    kp_s[SWA_BLK + SEQ:, :] = zeros_blk
    vc_s[...] = in_ref[0:CTX, vcols].astype(BF16)
    vp_s[0:SWA_BLK, :] = zeros_blk
    vp_s[SWA_BLK:SWA_BLK + SEQ, :] = in_ref[CTX:L, vcols].astype(BF16)
    vp_s[SWA_BLK + SEQ:, :] = zeros_blk

    a_idx = _iota((SWA_BLK, 3 * SWA_BLK), 0)
    s_idx = _iota((SWA_BLK, 3 * SWA_BLK), 1)
    in_window = jnp.abs(s_idx - SWA_BLK - a_idx) <= SWA_BLK

    def latent_block(n, carry):
        q0 = pl.multiple_of(CTX + n * SWA_BLK, SWA_BLK)
        k0 = pl.multiple_of(n * SWA_BLK, SWA_BLK)
        qblk = q_s[pl.ds(q0, SWA_BLK), :]
        kw = kp_s[pl.ds(k0, 3 * SWA_BLK), :]
        vw = vp_s[pl.ds(k0, 3 * SWA_BLK), :]
        kpos = n * SWA_BLK - SWA_BLK + s_idx
        mask = in_window & (kpos >= 0) & (kpos < SEQ)
        heads = range(SWA_QH)
        hsl = [slice(hq * SWA_DH, (hq + 1) * SWA_DH) for hq in heads]
        kvsl = [slice((hq // SWA_GROUP) * SWA_DH, (hq // SWA_GROUP + 1) * SWA_DH) for hq in heads]
        s_loc = [_dot_nt(qblk[:, hsl[hq]], kw[:, kvsl[hq]]) for hq in heads]
        s_ctx = [_dot_nt(qblk[:, hsl[hq]], kc_s[:, kvsl[hq]]) for hq in heads]
        p_loc, p_ctx, den = [], [], []
        for hq in heads:
            sink = sink_ref[hq]
            sl = jnp.where(mask, s_loc[hq], NEG_INF)
            m = jnp.maximum(jnp.maximum(jnp.max(sl, axis=-1, keepdims=True),
                                        jnp.max(s_ctx[hq], axis=-1, keepdims=True)), sink)
            pl_h = jnp.exp(sl - m)
            pc_h = jnp.exp(s_ctx[hq] - m)
            den.append(jnp.sum(pl_h, axis=-1, keepdims=True) + jnp.sum(pc_h, axis=-1, keepdims=True)
                       + jnp.exp(sink - m))
            p_loc.append(pl_h.astype(BF16))
            p_ctx.append(pc_h.astype(BF16))
        acc = [_dot(p_loc[hq], vw[:, kvsl[hq]]) + _dot(p_ctx[hq], vc_s[:, kvsl[hq]]) for hq in heads]
        for hq in heads:
            o_ref[pl.ds(q0, SWA_BLK), hsl[hq]] = acc[hq] / den[hq]
        return carry

    lax.fori_loop(0, SEQ // SWA_BLK, latent_block, 0)

    for n in range(CTX // SWA_BLK):
        rows = slice(n * SWA_BLK, (n + 1) * SWA_BLK)
        qblk = q_s[rows, :]
        for hq in range(SWA_QH):
            kv = hq // SWA_GROUP
            hs = slice(hq * SWA_DH, (hq + 1) * SWA_DH)
            kvs = slice(kv * SWA_DH, (kv + 1) * SWA_DH)
            sink = sink_ref[hq]
            s = _dot_nt(qblk[:, hs], kc_s[:, kvs])
            m = jnp.maximum(jnp.max(s, axis=-1, keepdims=True), sink)
            p = jnp.exp(s - m)
            den = jnp.sum(p, axis=-1, keepdims=True) + jnp.exp(sink - m)
            o_ref[rows, hs] = _dot(p.astype(BF16), vc_s[:, kvs]) / den


def _swa_call(swa, qg, kg, sink, cos_t, sin_t):
    bsz = swa.shape[0]
    wq = SWA_QH * SWA_DH

    def small(shape):
        nd = len(shape)
        return pl.BlockSpec(shape, lambda b: (0,) * nd)

    return pl.pallas_call(
        _swa_kernel,
        grid=(bsz,),
        in_specs=[pl.BlockSpec(memory_space=pltpu.SMEM),
                  pl.BlockSpec((None, L, MAIN_SPLITS[1]), lambda b: (b, 0, 0)),
                  small((1, 128)), small((1, 128)), small((SEQ, 128)), small((SEQ, 128))],
        out_specs=pl.BlockSpec((None, L, wq), lambda b: (b, 0, 0)),
        out_shape=jax.ShapeDtypeStruct((bsz, L, wq), F32),
        scratch_shapes=[pltpu.VMEM((L, wq), BF16),
                        pltpu.VMEM((SEQ + 2 * SWA_BLK, 128), BF16), pltpu.VMEM((SEQ + 2 * SWA_BLK, 128), BF16),
                        pltpu.VMEM((CTX, 128), BF16), pltpu.VMEM((CTX, 128), BF16)],
        compiler_params=_cparams(("arbitrary",), 48),
        name="swa",
    )(sink, swa, qg, kg, cos_t, sin_t)


LRU_RB = 32


def _lru_kernel(in_ref, cw_ref, cb_ref, wg_ref, gb_ref, lam_ref, o_ref, u_s, a_s, b_s):
    row = _iota((L, 1), 0)
    u_s[...] = _seg_conv4(in_ref[:, 0:LRU_W], cw_ref[...], row) + cb_ref[...]

    sub = _iota((1, 8, 1), 1)
    n_blk = L // LRU_RB
    n_ctx_blk = CTX // LRU_RB

    for d in range(2):
        sp = _softplus(-lam_ref[d])

        def coeffs(t, carry, d=d, sp=sp):
            rows = pl.ds(pl.multiple_of(t * TM, TM), TM)
            ub = u_s[rows, :]
            pre = _dot(ub.astype(BF16), wg_ref[d]) + gb_ref[d]
            r = _sigmoid(pre[:, 0:LRU_W])
            i = _sigmoid(pre[:, LRU_W:])
            log_a = -LRU_C * r * sp
            a = jnp.exp(log_a)
            a_s[rows, :] = a
            b_s[rows, :] = jnp.sqrt(jnp.tanh(-log_a) * (a * a + 1.0)) * (i * ub)
            return carry

        lax.fori_loop(0, N_TILES, coeffs, 0)

        def scan_block(t, hin, d=d):
            if d == 0:
                blk = t
            else:
                blk = jnp.where(t < n_ctx_blk, n_ctx_blk - 1 - t, n_blk + n_ctx_blk - 1 - t)
            rows = pl.ds(pl.multiple_of(blk * LRU_RB, LRU_RB), LRU_RB)
            a = a_s[rows, :].reshape(LRU_RB // 8, 8, LRU_W)
            b = b_s[rows, :].reshape(LRU_RB // 8, 8, LRU_W)
            for s in (1, 2, 4):
                if d == 0:
                    valid = sub >= s
                    shift = s
                else:
                    valid = sub < 8 - s
                    shift = 8 - s
                a_sh = jnp.where(valid, pltpu.roll(a, shift, 1), 1.0)
                b_sh = jnp.where(valid, pltpu.roll(b, shift, 1), 0.0)
                b = a * b_sh + b
                a = a * a_sh
            tiles = range(LRU_RB // 8) if d == 0 else range(LRU_RB // 8 - 1, -1, -1)
            hs = [None] * (LRU_RB // 8)
            for ti in tiles:
                ht = a[ti] * hin + b[ti]
                hs[ti] = ht
                hin = ht[7:8, :] if d == 0 else ht[0:1, :]
            hblk = jnp.concatenate(hs, axis=0)
            if d == 0:
                o_ref[rows, :] = hblk
            else:
                o_ref[rows, :] += hblk
            return hin

        lax.fori_loop(0, n_blk, scan_block, jnp.zeros((1, LRU_W), F32))

    y = in_ref[:, LRU_W:]
    gelu = 0.5 * y * (1.0 + jnp.tanh(np.sqrt(2.0 / np.pi) * (y + 0.044715 * (y * y * y))))
    o_ref[...] = o_ref[...] * gelu


def _lru_call(lru, conv_w, conv_b, wg, gb, lam):
    bsz = lru.shape[0]

    def small(shape):
        nd = len(shape)
        return pl.BlockSpec(shape, lambda b: (0,) * nd)

    return pl.pallas_call(
        _lru_kernel,
        grid=(bsz,),
        in_specs=[pl.BlockSpec((None, L, 2 * LRU_W), lambda b: (b, 0, 0)),
                  small((4, LRU_W)), small((1, LRU_W)), small((2, LRU_W, 2 * LRU_W)),
                  small((2, 1, 2 * LRU_W)), small((2, 1, LRU_W))],
        out_specs=pl.BlockSpec((None, L, LRU_W), lambda b: (b, 0, 0)),
        out_shape=jax.ShapeDtypeStruct((bsz, L, LRU_W), F32),
        scratch_shapes=[pltpu.VMEM((L, LRU_W), F32)] * 3,
        compiler_params=_cparams(("arbitrary",), 56),
        name="lru",
    )(lru, conv_w, conv_b, wg, gb, lam)


def _merge_kernel(a_ref, b_ref, c_ref, gl_ref, x_ref, mod_ref, g2_ref, wb_ref, wo_ref, wr_ref,
                  xn_ref, h2_ref, aff_ref):
    mixed = None
    for i, br in enumerate((a_ref, b_ref, c_ref)):
        term = _sigmoid(gl_ref[:, i * D:(i + 1) * D]) * _dot(br[...].astype(BF16), wb_ref[i])
        mixed = term if mixed is None else mixed + term
    mod = mod_ref[...]
    xn = x_ref[...] + mod[2:3] * _dot(mixed.astype(BF16), wo_ref[...])
    xn_ref[...] = xn
    h2 = _modulated_norm(xn, g2_ref[...], mod[3:4], mod[4:5])
    h2_ref[...] = h2.astype(BF16)
    logits = _dot3(h2, wr_ref[...])
    e = jnp.exp(logits - jnp.max(logits, axis=-1, keepdims=True))
    aff_ref[...] = e / jnp.sum(e, axis=-1, keepdims=True)


def _merge_call(a_br, b_br, c_br, gate, xt, modsel, g2, wb, wo, wr, layer):
    bsz = xt.shape[0]
    return pl.pallas_call(
        _merge_kernel,
        grid=(bsz, N_TILES),
        in_specs=[_tile_spec(512), _tile_spec(512), _tile_spec(512), _tile_spec(3 * D), _tile_spec(D),
                  _mod_spec(), _const_spec((1, D)), _layer_spec((3, 512, D), layer), _layer_spec((D, D), layer),
                  _const_spec((D, N_EXP))],
        out_specs=[_tile_spec(D), _tile_spec(D), _tile_spec(N_EXP)],
        out_shape=[jax.ShapeDtypeStruct((bsz, L, D), F32), jax.ShapeDtypeStruct((bsz, L, D), BF16),
                   jax.ShapeDtypeStruct((bsz, L, N_EXP), F32)],
        compiler_params=_cparams(("arbitrary", "arbitrary"), 40),
        name="merge",
    )(a_br, b_br, c_br, gate, xt, modsel, g2, wb, wo, wr)


def _excl_prefix(m):
    w = 256
    pp = _iota((w, w), 0)
    jj = _iota((w, w), 1)
    upper = jnp.where(pp < jj, 1.0, 0.0).astype(BF16)
    off = jnp.zeros((m.shape[0], 1), F32)
    parts = []
    for c0 in range(0, m.shape[1], w):
        blk = m[:, c0:c0 + w]
        parts.append(_dot(blk.astype(BF16), upper) + off)
        off = off + jnp.sum(blk, axis=1, keepdims=True)
    return parts[0] if len(parts) == 1 else jnp.concatenate(parts, axis=1)


def _topk_pos(v, k):
    bits = pltpu.bitcast(v, I32)
    thr = jnp.zeros((v.shape[0], 1), I32)
    for bit in range(30, -1, -1):
        cand = thr | (1 << bit)
        cnt = jnp.sum(jnp.where(bits >= cand, 1.0, 0.0), axis=1, keepdims=True)
        thr = jnp.where(cnt >= k, cand, thr)
    gt = bits > thr
    eq = jnp.where(bits == thr, 1.0, 0.0)
    need = k - jnp.sum(jnp.where(gt, 1.0, 0.0), axis=1, keepdims=True)
    sel = jnp.where(gt, 1.0, jnp.where(_excl_prefix(eq) < need, eq, 0.0))
    return jnp.where(sel > 0.0, _excl_prefix(sel), -1.0).astype(I32)


def _select_kernel(aff_ref, pos_ref):
    pos_ref[:, 0:CTX] = _topk_pos(aff_ref[:, 0:CTX], CAP_C)
    pos_ref[:, CTX:L] = _topk_pos(aff_ref[:, CTX:L], CAP_X)


def _select_call(aff_t):
    bsz = aff_t.shape[0]
    return pl.pallas_call(
        _select_kernel,
        grid=(bsz,),
        in_specs=[pl.BlockSpec((None, N_EXP, L), lambda b: (b, 0, 0))],
        out_specs=pl.BlockSpec((None, N_EXP, L), lambda b: (b, 0, 0)),
        out_shape=jax.ShapeDtypeStruct((bsz, N_EXP, L), I32),
        compiler_params=_cparams(("arbitrary",), 32),
        name="select",
    )(aff_t)


GATHER_EXPERTS = 4


def _gather_kernel(pos_ref, aff_ref, h_ref, xs_ref, xc_ref, gx_ref, gc_ref):
    for lo, hi, cap, x_ref, g_ref in ((CTX, L, CAP_X, xs_ref, gx_ref), (0, CTX, CAP_C, xc_ref, gc_ref)):
        ranks = _iota((cap, hi - lo), 0)
        hits = [ranks == pos_ref[g][:, lo:hi] for g in range(GATHER_EXPERTS)]
        onehot = jnp.concatenate([jnp.where(hit, 1.0, 0.0).astype(BF16) for hit in hits], axis=0)
        rows = _dot(onehot, h_ref[lo:hi, :]).astype(BF16)
        for g in range(GATHER_EXPERTS):
            x_ref[g] = rows[g * cap:(g + 1) * cap, :]
            g_ref[g] = jnp.sum(jnp.where(hits[g], aff_ref[g][:, lo:hi], 0.0), axis=1, keepdims=True)


def _gather_call(pos4, aff4, h2):
    bsz = h2.shape[0]
    row_spec = pl.BlockSpec((None, GATHER_EXPERTS, 1, L), lambda b, e: (b, e, 0, 0))

    def cap_spec(cap, width):
        return pl.BlockSpec((GATHER_EXPERTS, cap, width), lambda b, e: (e, b, 0))

    return pl.pallas_call(
        _gather_kernel,
        grid=(bsz, N_EXP // GATHER_EXPERTS),
        in_specs=[row_spec, row_spec, pl.BlockSpec((None, L, D), lambda b, e: (b, 0, 0))],
        out_specs=[cap_spec(CAP_X, D), cap_spec(CAP_C, D), cap_spec(CAP_X, 1), cap_spec(CAP_C, 1)],
        out_shape=[jax.ShapeDtypeStruct((N_EXP, bsz * CAP_X, D), BF16),
                   jax.ShapeDtypeStruct((N_EXP, bsz * CAP_C, D), BF16),
                   jax.ShapeDtypeStruct((N_EXP, bsz * CAP_X, 1), F32),
                   jax.ShapeDtypeStruct((N_EXP, bsz * CAP_C, 1), F32)],
        compiler_params=_cparams(("arbitrary", "arbitrary"), 48),
        name="gather",
    )(pos4, aff4, h2)


FFN_HC = 512
FFN_MT = 512


def _ffn_kernel(xs_ref, xc_ref, gx_ref, gc_ref, wg_ref, wu_ref, wd_ref, ys_ref, yc_ref,
                acc_x, acc_c, wgb, wub, wdb):
    hc = pl.program_id(1)
    wgb[...] = wg_ref[...].astype(BF16)
    wub[...] = wu_ref[...].astype(BF16)
    wdb[...] = wd_ref[...].astype(BF16)

    def ffn(x):
        g = _dot(x, wgb[...])
        hid = _silu(g) * _dot(x, wub[...])
        return _dot(hid.astype(BF16), wdb[...])

    @pl.when(hc == 0)
    def _():
        acc_x[...] = jnp.zeros_like(acc_x)
        acc_c[...] = jnp.zeros_like(acc_c)

    for m0 in range(0, xs_ref.shape[0], FFN_MT):
        acc_x[m0:m0 + FFN_MT, :] += ffn(xs_ref[m0:m0 + FFN_MT, :])
    acc_c[...] += ffn(xc_ref[...])

    @pl.when(hc == pl.num_programs(1) - 1)
    def _():
        ys_ref[...] = (acc_x[...] * gx_ref[...]).astype(BF16)
        yc_ref[...] = (acc_c[...] * gc_ref[...]).astype(BF16)


def _ffn_call(xs, xc, gx, gc, w_gate, w_up, w_down, layer):
    mx, mc = xs.shape[1], xc.shape[1]
    return pl.pallas_call(
        _ffn_kernel,
        grid=(N_EXP, EXP_HID // FFN_HC),
        in_specs=[pl.BlockSpec((None, mx, D), lambda e, j: (e, 0, 0)),
                  pl.BlockSpec((None, mc, D), lambda e, j: (e, 0, 0)),
                  pl.BlockSpec((None, mx, 1), lambda e, j: (e, 0, 0)),
                  pl.BlockSpec((None, mc, 1), lambda e, j: (e, 0, 0)),
                  pl.BlockSpec((None, None, D, FFN_HC), lambda e, j: (layer, e, 0, j)),
                  pl.BlockSpec((None, None, D, FFN_HC), lambda e, j: (layer, e, 0, j)),
                  pl.BlockSpec((None, None, FFN_HC, D), lambda e, j: (layer, e, j, 0))],
        out_specs=[pl.BlockSpec((None, mx, D), lambda e, j: (e, 0, 0)),
                   pl.BlockSpec((None, mc, D), lambda e, j: (e, 0, 0))],
        out_shape=[jax.ShapeDtypeStruct(xs.shape, BF16), jax.ShapeDtypeStruct(xc.shape, BF16)],
        scratch_shapes=[pltpu.VMEM((mx, D), F32), pltpu.VMEM((mc, D), F32),
                        pltpu.VMEM((D, FFN_HC), BF16), pltpu.VMEM((D, FFN_HC), BF16),
                        pltpu.VMEM((FFN_HC, D), BF16)],
        compiler_params=_cparams(("arbitrary", "arbitrary"), 56),
        name="ffn",
    )(xs, xc, gx, gc, w_gate, w_up, w_down)


SCATTER_TM = 768


def _scatter_kernel(ys_ref, yc_ref, pos_ref, x_ref, g2_ref, o_ref):
    t = pl.program_id(1)

    def scattered(r0, r1, y_ref, cap):
        pos = pos_ref[r0:r1, :]
        ranks = _iota((r1 - r0, cap), 1)
        onehot = jnp.concatenate(
            [jnp.where(pos[:, e:e + 1] == ranks, 1.0, 0.0).astype(BF16) for e in range(N_EXP)], axis=1)
        return _dot(onehot, y_ref[...].reshape(N_EXP * cap, D))

    @pl.when(t == 0)
    def _():
        o_ref[0:CTX, :] = x_ref[0:CTX, :] + g2_ref[0] * scattered(0, CTX, yc_ref, CAP_C)
        o_ref[CTX:, :] = x_ref[CTX:, :] + g2_ref[1] * scattered(CTX, SCATTER_TM, ys_ref, CAP_X)

    @pl.when(t > 0)
    def _():
        o_ref[...] = x_ref[...] + g2_ref[1] * scattered(0, SCATTER_TM, ys_ref, CAP_X)


def _scatter_call(ys, yc, pos_t, xn, gate2):
    bsz = xn.shape[0]

    def rows_spec(width):
        return pl.BlockSpec((None, SCATTER_TM, width), lambda b, t: (b, t, 0))

    return pl.pallas_call(
        _scatter_kernel,
        grid=(bsz, L // SCATTER_TM),
        in_specs=[pl.BlockSpec((N_EXP, CAP_X, D), lambda b, t: (0, b, 0)),
                  pl.BlockSpec((N_EXP, CAP_C, D), lambda b, t: (0, b, 0)),
                  rows_spec(N_EXP), rows_spec(D),
                  pl.BlockSpec((None, 2, 1, D), lambda b, t: (b, 0, 0, 0))],
        out_specs=rows_spec(D),
        out_shape=jax.ShapeDtypeStruct((bsz, L, D), F32),
        compiler_params=_cparams(("arbitrary", "arbitrary"), 56),
        name="scatter",
    )(ys, yc, pos_t, xn, gate2)


def _rope_tables():
    rows = SEQ // GRID_W
    row = np.repeat(np.arange(rows, dtype=np.float32), GRID_W)
    col = np.tile(np.arange(GRID_W, dtype=np.float32), rows)
    axis_dim = SWA_DH // 2
    inv_freq = (np.float32(ROPE_THETA) ** (-np.arange(0, axis_dim, 2, dtype=np.float32) / np.float32(axis_dim)))
    inv_freq = inv_freq.astype(np.float32)
    ang = np.concatenate([row[:, None] * inv_freq, col[:, None] * inv_freq], axis=-1).astype(np.float32)
    ang = np.tile(np.repeat(ang, 2, axis=-1), (1, 128 // SWA_DH))
    return jnp.asarray(np.cos(ang), F32), jnp.asarray(np.sin(ang), F32)


def _block_diag(w):
    n, bw, _ = w.shape
    eye = jnp.eye(n, dtype=w.dtype)
    return (eye[:, None, :, None] * w[:, :, None, :]).reshape(n * bw, n * bw)


def _pad16(v):
    return jnp.concatenate([v.reshape(-1), jnp.zeros((16 - v.size,), v.dtype)])


def _layer(xt, mods_l, layer, norm1_g, norm2_g, w_main, w_ab, gdn_conv_w, gdn_a_log, gdn_dt_bias, gdn_onorm_g,
           swa_qnorm_g, swa_knorm_g, swa_sink, lru_conv_w, lru_conv_b, lru_gate_w, lru_gate_b, lru_lambda,
           w_branch, w_out, w_router, w_exp_gate, w_exp_up, w_exp_down, cos_t, sin_t):
    bsz = xt.shape[0]
    mods6 = mods_l.reshape(16, 6, D)
    modsel = jnp.stack([jnp.broadcast_to(mods6[8], (bsz, 6, D)), mods6[:bsz]], axis=1)

    gdn, swa, lru, gate, ab = _inproj_call(xt, modsel, norm1_g.reshape(1, D), w_main, w_ab, layer)

    abt = jnp.swapaxes(ab.reshape(bsz, N_CHUNK, GDN_C, 16), 2, 3)
    alog16, dt16 = _pad16(gdn_a_log), _pad16(gdn_dt_bias)
    a_br = _gdn_call(gdn, ab, abt, gdn_conv_w, alog16.reshape(1, 16), dt16.reshape(1, 16),
                     alog16.reshape(16, 1), dt16.reshape(16, 1), gdn_onorm_g.reshape(1, GDN_DK))

    b_br = _swa_call(swa, jnp.tile(swa_qnorm_g, 128 // SWA_DH).reshape(1, 128),
                     jnp.tile(swa_knorm_g, 128 // SWA_DH).reshape(1, 128), swa_sink, cos_t, sin_t)

    wg = jnp.stack([jnp.concatenate([_block_diag(lru_gate_w[d, 0]), _block_diag(lru_gate_w[d, 1])], axis=1)
                    for d in range(2)]).astype(BF16)
    c_br = _lru_call(lru, lru_conv_w, lru_conv_b.reshape(1, LRU_W), wg,
                     lru_gate_b.reshape(2, 1, 2 * LRU_W), lru_lambda.reshape(2, 1, LRU_W))

    xn, h2, aff = _merge_call(a_br, b_br, c_br, gate, xt, modsel, norm2_g.reshape(1, D),
                              w_branch, w_out, w_router, layer)

    aff_t = jnp.swapaxes(aff, 1, 2)
    pos = _select_call(aff_t)
    xs, xc, gx, gc = _gather_call(pos.reshape(bsz, N_EXP, 1, L), aff_t.reshape(bsz, N_EXP, 1, L), h2)
    ys, yc = _ffn_call(xs, xc, gx, gc, w_exp_gate, w_exp_up, w_exp_down, layer)
    gate2 = modsel[:, :, 5:6, :]
    return _scatter_call(ys, yc, jnp.swapaxes(pos, 1, 2), xn, gate2)


def kernel(x, c, ctx, c_ctx, w_ada, b_ada, norm1_g, norm2_g, w_in, gdn_conv_w, gdn_a_log, gdn_dt_bias, gdn_onorm_g, swa_qnorm_g, swa_knorm_g, swa_sink, lru_conv_w, lru_conv_b, lru_gate_w, lru_gate_b, lru_lambda, w_branch, w_out, w_router, w_exp_gate, w_exp_up, w_exp_down):
    bsz = x.shape[0]
    depth = w_ada.shape[0]
    assert x.shape == (bsz, SEQ, D) and ctx.shape == (bsz, CTX, D) and bsz <= 8
    cond16 = jnp.concatenate([c, jnp.zeros((8 - bsz, D), F32), c_ctx[None, :], jnp.zeros((7, D), F32)], axis=0)
    mods = _ada_call(cond16, w_ada, b_ada)
    cos_t, sin_t = _rope_tables()
    xt = jnp.concatenate([ctx, x], axis=1)
    w_main = jnp.concatenate([w_in[:, :, :2048], w_in[:, :, 2064:]], axis=2).astype(BF16)
    w_ab = w_in[:, :, 2048:2064].astype(BF16)
    w_branch, w_out = w_branch.astype(BF16), w_out.astype(BF16)
    for l in range(depth):
        xt = _layer(xt, mods[l], l, norm1_g[l], norm2_g[l], w_main, w_ab, gdn_conv_w[l], gdn_a_log[l],
                    gdn_dt_bias[l], gdn_onorm_g[l], swa_qnorm_g[l], swa_knorm_g[l], swa_sink[l],
                    lru_conv_w[l], lru_conv_b[l], lru_gate_w[l], lru_gate_b[l], lru_lambda[l],
                    w_branch, w_out, w_router[l], w_exp_gate, w_exp_up, w_exp_down, cos_t, sin_t)
    return xt[:, CTX:, :]
```

```python
import functools

import jax
import jax.numpy as jnp
import numpy as np
from jax import lax
from jax.experimental import pallas as pl
from jax.experimental.pallas import tpu as pltpu

F32 = jnp.float32
BF16 = jnp.bfloat16
I32 = jnp.int32

D = 1024
SEQ = 2048
CTX = 256
L = CTX + SEQ
GRID_W = 64
EPS = 1e-6
NEG_INF = -1e30
ROPE_THETA = 10000.0

GDN_H = 4
GDN_DK = 128
GDN_C = 64
N_CHUNK = L // GDN_C
N_CTX_CHUNK = CTX // GDN_C

SWA_QH = 8
SWA_KVH = 2
SWA_GROUP = SWA_QH // SWA_KVH
SWA_DH = 64
SWA_BLK = 128

LRU_W = 512
LRU_NBLK = 8
LRU_C = 8.0

N_EXP = 16
EXP_HID = 2048
CAP_X = 2 * SEQ // N_EXP
CAP_C = 2 * CTX // N_EXP

TM = 256
N_TILES = L // TM
MAIN_SPLITS = (2048, 768, 1024, 3072)
N_MAIN = sum(MAIN_SPLITS)

VMEM_MIB_V7X = 64


def _cparams(sem, vmem_mib):
    assert vmem_mib < VMEM_MIB_V7X
    return pltpu.CompilerParams(dimension_semantics=sem, vmem_limit_bytes=vmem_mib * 1024 * 1024)


def _dot(a, b):
    return jnp.dot(a, b, preferred_element_type=F32)


def _dot_nt(a, b):
    return lax.dot_general(a, b, (((1,), (1,)), ((), ())), preferred_element_type=F32)


def _dot_tn(a, b):
    return lax.dot_general(a, b, (((0,), (0,)), ((), ())), preferred_element_type=F32)


def _split(x):
    hi = x.astype(BF16)
    lo = (x - hi.astype(F32)).astype(BF16)
    return hi, lo


def _dot3(a, b):
    ah, al = _split(a)
    bh, bl = _split(b)
    return _dot(ah, bh) + _dot(al, bh) + _dot(ah, bl)


def _sigmoid(x):
    return 0.5 * jnp.tanh(0.5 * x) + 0.5


def _silu(x):
    return x * _sigmoid(x)


def _softplus(x):
    return jnp.maximum(x, 0.0) + jnp.log1p(jnp.exp(-jnp.abs(x)))


def _iota(shape, dim):
    return lax.broadcasted_iota(I32, shape, dim)


def _seg_conv4(x, w, row):
    n = x.shape[0]
    segpos = jnp.where(row < CTX, row, row - CTX)
    is_last = (row == CTX - 1) | (row == n - 1)
    xm2 = jnp.where(segpos >= 2, pltpu.roll(x, 2, 0), 0.0)
    xm1 = jnp.where(segpos >= 1, pltpu.roll(x, 1, 0), 0.0)
    xp1 = jnp.where(is_last, 0.0, pltpu.roll(x, n - 1, 0))
    return xm2 * w[0:1] + xm1 * w[1:2] + x * w[2:3] + xp1 * w[3:4]


def _ada_kernel(c_ref, w_ref, b_ref, o_ref):
    o_ref[...] = _dot3(_silu(c_ref[...]), w_ref[...]) + b_ref[...]


def _ada_call(cond16, w_ada, b_ada):
    depth, _, n = w_ada.shape
    tn = 1536
    return pl.pallas_call(
        _ada_kernel,
        grid=(depth, n // tn),
        in_specs=[pl.BlockSpec((16, D), lambda l, j: (0, 0)),
                  pl.BlockSpec((None, D, tn), lambda l, j: (l, 0, j)),
                  pl.BlockSpec((None, 1, tn), lambda l, j: (l, 0, j))],
        out_specs=pl.BlockSpec((None, 16, tn), lambda l, j: (l, 0, j)),
        out_shape=jax.ShapeDtypeStruct((depth, 16, n), F32),
        compiler_params=_cparams(("arbitrary", "arbitrary"), 32),
        name="ada",
    )(cond16, w_ada, b_ada.reshape(depth, 1, n))


def _modulated_norm(x, g, shift, scale):
    xn = x * lax.rsqrt(jnp.mean(x * x, axis=-1, keepdims=True) + EPS) * g
    return xn * (1.0 + scale) + shift


def _inproj_kernel(x_ref, mod_ref, g_ref, wgdn_ref, wrest_ref, wab_ref, gdn_ref, swa_ref, lru_ref, gate_ref, ab_ref):
    mod = mod_ref[...]
    hb = _modulated_norm(x_ref[...], g_ref[...], mod[0:1], mod[1:2]).astype(BF16)
    col = 0
    for ref, width in zip((gdn_ref, swa_ref, lru_ref, gate_ref), MAIN_SPLITS):
        w_ref = wgdn_ref if ref is gdn_ref else wrest_ref
        for c0 in range(0, width, 512):
            cw = min(512, width - c0)
            ref[:, c0:c0 + cw] = _dot(hb, w_ref[:, col + c0:col + c0 + cw])
        col = 0 if ref is gdn_ref else col + width
    ab_ref[...] = _dot(hb, wab_ref[...])


def _mod_spec():
    return pl.BlockSpec((None, None, 6, D), lambda b, t: (b, jnp.minimum(t, 1), 0, 0))


def _tile_spec(width):
    return pl.BlockSpec((None, TM, width), lambda b, t: (b, t, 0))


def _const_spec(shape):
    nd = len(shape)
    return pl.BlockSpec(shape, lambda b, t: (0,) * nd, pipeline_mode=pl.Buffered(1))


def _layer_spec(shape, layer):
    nd = len(shape)
    return pl.BlockSpec((None,) + shape, lambda b, t: (layer,) + (0,) * nd, pipeline_mode=pl.Buffered(1))


def _inproj_call(xt, modsel, g1, w_gdn, w_rest, w_ab, layer):
    bsz = xt.shape[0]
    outs = [jax.ShapeDtypeStruct((bsz, L, w), F32) for w in MAIN_SPLITS] + [jax.ShapeDtypeStruct((bsz, L, 16), F32)]
    return pl.pallas_call(
        _inproj_kernel,
        grid=(bsz, N_TILES),
        in_specs=[_tile_spec(D), _mod_spec(), _const_spec((1, D)),
                  _layer_spec((D, MAIN_SPLITS[0]), layer), _layer_spec((D, N_MAIN - MAIN_SPLITS[0]), layer),
                  _layer_spec((D, 16), layer)],
        out_specs=[_tile_spec(w) for w in MAIN_SPLITS] + [_tile_spec(16)],
        out_shape=outs,
        compiler_params=_cparams(("arbitrary", "arbitrary"), 48),
        name="inproj",
    )(xt, modsel, g1, w_gdn, w_rest, w_ab)


GDN_HPS = 2
GDN_PREP_CHUNKS = 6


def _gdn_kernel(q_ref, k_ref, v_ref, z_ref, cwq_ref, cwk_ref, cwv_ref, ab_ref, abt_ref,
                alog_r_ref, dt_r_ref, alog_c_ref, dt_c_ref, og_ref, o_ref,
                qs, ks, vs, kq_s, n_s, eg_s, orev_s):
    head0 = pl.program_id(1) * GDN_HPS
    row = _iota((L, 1), 0)

    for hh in range(GDN_HPS):
        hc = slice(hh * GDN_DK, (hh + 1) * GDN_DK)
        q = _silu(_seg_conv4(q_ref[:, hc], cwq_ref[:, hc], row))
        qs[hh] = q * lax.rsqrt(jnp.sum(q * q, axis=-1, keepdims=True) + EPS) * (GDN_DK ** -0.5)
        k = _silu(_seg_conv4(k_ref[:, hc], cwk_ref[:, hc], row))
        ks[hh] = k * lax.rsqrt(jnp.sum(k * k, axis=-1, keepdims=True) + EPS)
        vs[hh] = _silu(_seg_conv4(v_ref[:, hc], cwv_ref[:, hc], row))

    c = GDN_C
    ii = _iota((c, c), 0)
    jj = _iota((c, c), 1)
    eye = jnp.where(ii == jj, 1.0, 0.0)
    blk = [(ii // b) == (jj // b) for b in (8, 16, 32, 64)]
    lane16 = _iota((c, 16), 1)
    sub16 = _iota((16, c), 0)

    def obuf(hh, d):
        return o_ref.at[:, hh * GDN_DK:(hh + 1) * GDN_DK] if d == 0 else orev_s.at[hh]

    def prep(it, carry):
        chunks = []
        for u in range(GDN_PREP_CHUNKS):
            ci = it * GDN_PREP_CHUNKS + u
            rows = pl.ds(pl.multiple_of(ci * c, c), c)
            ab = ab_ref[rows, :]
            g_all = -jnp.exp(alog_r_ref[...]) * _softplus(ab + dt_r_ref[...])
            beta_all = _sigmoid(ab)
            g_all_t = -jnp.exp(alog_c_ref[...]) * _softplus(abt_ref[ci] + dt_c_ref[...])
            for hh in range(GDN_HPS):
                qc, kc, vc = qs[hh, rows, :], ks[hh, rows, :], vs[hh, rows, :]
                chunks.append(dict(ci=ci, rows=rows, hh=hh, qc=qc, kc=kc, vc=vc, g_all=g_all, beta_all=beta_all,
                                   g_all_t=g_all_t, kcb=kc.astype(BF16), qcb=qc.astype(BF16)))
        for ch in chunks:
            ch["kk"] = _dot_nt(ch["kcb"], ch["kcb"])
        for ch in chunks:
            ch["qk"] = _dot_nt(ch["qcb"], ch["kcb"])
        chains = []
        for ch in chunks:
            g_all, beta_all, g_all_t = ch["g_all"], ch["beta_all"], ch["g_all_t"]
            for d in range(2):
                if d == 0:
                    incl, strict, incl_t = jj <= ii, jj < ii, ii <= jj
                else:
                    incl, strict, incl_t = jj >= ii, jj > ii, ii >= jj
                a_idx = d * GDN_H + head0 + ch["hh"]
                gcol = jnp.sum(jnp.where(lane16 == a_idx, g_all, 0.0), axis=1, keepdims=True)
                bcol = jnp.sum(jnp.where(lane16 == 2 * GDN_H + a_idx, beta_all, 0.0), axis=1, keepdims=True)
                grow = jnp.sum(jnp.where(sub16 == a_idx, g_all_t, 0.0), axis=0, keepdims=True)
                gbc = jnp.sum(jnp.where(incl, grow, 0.0), axis=1, keepdims=True)
                gbr = jnp.sum(jnp.where(incl_t, gcol, 0.0), axis=0, keepdims=True)
                decay = jnp.where(incl, jnp.exp(jnp.where(incl, gbc - gbr, 0.0)), 0.0)
                a_mat = jnp.where(strict, ch["kk"] * bcol * decay, 0.0)
                egbc = jnp.exp(gbc)
                glast = gbc[c - 1:c, :] if d == 0 else gbc[0:1, :]
                chains.append(dict(
                    ch=ch, d=d, a=a_mat, egbc=egbc, eglast=jnp.exp(glast),
                    rhs=jnp.concatenate([ch["vc"] * bcol, ch["kc"] * (bcol * egbc)], axis=1).astype(BF16),
                    kd=(ch["kc"] * jnp.exp(glast - gbc)).astype(BF16),
                    qkd=(ch["qk"] * decay).astype(BF16)))
        for cn in chains:
            a8 = jnp.where(blk[0], cn["a"], 0.0)
            a8b = a8.astype(BF16)
            cn["p2"] = _dot(a8b, a8b)
            cn["ima"] = (eye - a8).astype(BF16)
        for cn in chains:
            p2b = cn["p2"].astype(BF16)
            cn["p4"] = _dot(p2b, p2b)
            cn["inv"] = _dot(cn["ima"], (eye + cn["p2"]).astype(BF16))
        for cn in chains:
            cn["inv"] = _dot(cn["inv"].astype(BF16), (eye + cn["p4"]).astype(BF16))
        for lvl in range(1, len(blk)):
            for cn in chains:
                off = jnp.where(blk[lvl] & jnp.logical_not(blk[lvl - 1]), cn["a"], 0.0)
                cn["ot"] = _dot(off.astype(BF16), cn["inv"].astype(BF16))
            for cn in chains:
                cn["inv"] = cn["inv"] - _dot(cn["inv"].astype(BF16), cn["ot"].astype(BF16))
        for cn in chains:
            cn["uw"] = _dot(cn["inv"].astype(BF16), cn["rhs"]).astype(BF16)
        for cn in chains:
            cn["kn"] = _dot_tn(cn["kd"], cn["uw"])
        for cn in chains:
            cn["qo"] = _dot(cn["qkd"], cn["uw"])
        for cn in chains:
            d, ci, rows, hh = cn["d"], cn["ch"]["ci"], cn["ch"]["rows"], cn["ch"]["hh"]
            kq_s[hh, d, ci, 0:GDN_DK, :] = cn["kn"][:, GDN_DK:].astype(BF16)
            kq_s[hh, d, ci, GDN_DK:, :] = (cn["ch"]["qc"] * cn["egbc"] - cn["qo"][:, GDN_DK:]).astype(BF16)
            n_s[hh, d, ci] = cn["kn"][:, 0:GDN_DK]
            obuf(hh, d)[rows, :] = cn["qo"][:, 0:GDN_DK]
            eg_s[hh, d, ci] = jnp.broadcast_to(cn["eglast"], (8, GDN_DK))
        return carry

    lax.fori_loop(0, N_CHUNK // GDN_PREP_CHUNKS, prep, 0)

    def step(t, states):
        ci_rev = jnp.where(t < N_CTX_CHUNK, N_CTX_CHUNK - 1 - t, N_CHUNK + N_CTX_CHUNK - 1 - t)
        cis = (t, ci_rev)
        chains = [(hh, d) for hh in range(GDN_HPS) for d in range(2)]
        rs = [_dot(kq_s[hh, d, cis[d]], states[i].astype(BF16)) for i, (hh, d) in enumerate(chains)]
        new_states = []
        for i, (hh, d) in enumerate(chains):
            rows = pl.ds(pl.multiple_of(cis[d] * c, c), c)
            obuf(hh, d)[rows, :] += rs[i][GDN_DK:, :]
            new_states.append(states[i] * eg_s[hh, d, cis[d]][0:1, :] + n_s[hh, d, cis[d]] - rs[i][0:GDN_DK, :])
        return tuple(new_states)

    zero_state = jnp.zeros((GDN_DK, GDN_DK), F32)
    lax.fori_loop(0, N_CHUNK, step, (zero_state,) * (2 * GDN_HPS))

    for hh in range(GDN_HPS):
        hc = slice(hh * GDN_DK, (hh + 1) * GDN_DK)
        o = o_ref[:, hc] + orev_s[hh]
        y = o * lax.rsqrt(jnp.mean(o * o, axis=-1, keepdims=True) + EPS) * og_ref[...]
        o_ref[:, hc] = y * _silu(z_ref[:, hc])


def _gdn_call(gdn, ab, abt, conv_w, alog_r, dt_r, alog_c, dt_c, onorm_g):
    bsz = gdn.shape[0]

    wblk = GDN_HPS * GDN_DK
    nblk = GDN_H // GDN_HPS

    def head_spec(part):
        return pl.BlockSpec((None, L, wblk), lambda b, h: (b, 0, part * nblk + h))

    def cw_spec(part):
        return pl.BlockSpec((4, wblk), lambda b, h: (0, part * nblk + h))

    def small(shape):
        nd = len(shape)
        return pl.BlockSpec(shape, lambda b, h: (0,) * nd)

    seq = pltpu.VMEM((GDN_HPS, L, GDN_DK), F32)
    return pl.pallas_call(
        _gdn_kernel,
        grid=(bsz, nblk),
        in_specs=[head_spec(0), head_spec(1), head_spec(2), head_spec(3),
                  cw_spec(0), cw_spec(1), cw_spec(2),
                  pl.BlockSpec((None, L, 16), lambda b, h: (b, 0, 0)),
                  pl.BlockSpec((None, N_CHUNK, 16, GDN_C), lambda b, h: (b, 0, 0, 0)),
                  small((1, 16)), small((1, 16)), small((16, 1)), small((16, 1)), small((1, GDN_DK))],
        out_specs=pl.BlockSpec((None, L, wblk), lambda b, h: (b, 0, h)),
        out_shape=jax.ShapeDtypeStruct((bsz, L, GDN_H * GDN_DK), F32),
        scratch_shapes=[seq, seq, seq,
                        pltpu.VMEM((GDN_HPS, 2, N_CHUNK, GDN_DK + GDN_C, GDN_DK), BF16),
                        pltpu.VMEM((GDN_HPS, 2, N_CHUNK, GDN_DK, GDN_DK), F32),
                        pltpu.VMEM((GDN_HPS, 2, N_CHUNK, 8, GDN_DK), F32), seq],
        compiler_params=_cparams(("arbitrary", "arbitrary"), 60),
        name="gdn",
    )(gdn, gdn, gdn, gdn, conv_w, conv_w, conv_w, ab, abt, alog_r, dt_r, alog_c, dt_c, onorm_g)


def _swa_kernel(sink_ref, in_ref, qg_ref, kg_ref, cos_ref, sin_ref, o_ref, q_s, kp_s, vp_s, kc_s, vc_s):
    ii = _iota((128, 128), 0)
    jj = _iota((128, 128), 1)
    head_ones = jnp.where((ii // SWA_DH) == (jj // SWA_DH), 1.0, 0.0).astype(BF16)
    even = (_iota((1, 128), 1) % 2) == 0

    def norm(t, g):
        ssq = _dot((t * t).astype(BF16), head_ones)
        return t * lax.rsqrt(ssq * (1.0 / SWA_DH) + EPS) * g

    def rope(t):
        rot = jnp.where(even, -pltpu.roll(t, 127, 1), pltpu.roll(t, 1, 1))
        return t * cos_ref[...] + rot * sin_ref[...]

    scale = SWA_DH ** -0.5
    for cg in range(SWA_QH * SWA_DH // 128):
        cols = slice(cg * 128, (cg + 1) * 128)
        q_s[0:CTX, cols] = (norm(in_ref[0:CTX, cols], qg_ref[...]) * scale).astype(BF16)
        q_s[CTX:L, cols] = (rope(norm(in_ref[CTX:L, cols], qg_ref[...])) * scale).astype(BF16)
    kcols = slice(512, 640)
    vcols = slice(640, 768)
    zeros_blk = jnp.zeros((SWA_BLK, 128), BF16)
    kc_s[...] = norm(in_ref[0:CTX, kcols], kg_ref[...]).astype(BF16)
    kp_s[0:SWA_BLK, :] = zeros_blk
    kp_s[SWA_BLK:SWA_BLK + SEQ, :] = rope(norm(in_ref[CTX:L, kcols], kg_ref[...])).astype(BF16)
    kp_s[SWA_BLK + SEQ:, :] = zeros_blk
    vc_s[...] = in_ref[0:CTX, vcols].astype(BF16)
    vp_s[0:SWA_BLK, :] = zeros_blk
    vp_s[SWA_BLK:SWA_BLK + SEQ, :] = in_ref[CTX:L, vcols].astype(BF16)
    vp_s[SWA_BLK + SEQ:, :] = zeros_blk

    a_idx = _iota((SWA_BLK, 3 * SWA_BLK), 0)
    s_idx = _iota((SWA_BLK, 3 * SWA_BLK), 1)
    in_window = jnp.abs(s_idx - SWA_BLK - a_idx) <= SWA_BLK

    def latent_block(n, carry):
        q0 = pl.multiple_of(CTX + n * SWA_BLK, SWA_BLK)
        k0 = pl.multiple_of(n * SWA_BLK, SWA_BLK)
        qblk = q_s[pl.ds(q0, SWA_BLK), :]
        kw = kp_s[pl.ds(k0, 3 * SWA_BLK), :]
        vw = vp_s[pl.ds(k0, 3 * SWA_BLK), :]
        kpos = n * SWA_BLK - SWA_BLK + s_idx
        mask = in_window & (kpos >= 0) & (kpos < SEQ)
        heads = range(SWA_QH)
        hsl = [slice(hq * SWA_DH, (hq + 1) * SWA_DH) for hq in heads]
        kvsl = [slice((hq // SWA_GROUP) * SWA_DH, (hq // SWA_GROUP + 1) * SWA_DH) for hq in heads]
        s_loc = [_dot_nt(qblk[:, hsl[hq]], kw[:, kvsl[hq]]) for hq in heads]
        s_ctx = [_dot_nt(qblk[:, hsl[hq]], kc_s[:, kvsl[hq]]) for hq in heads]
        p_loc, p_ctx, den = [], [], []
        for hq in heads:
            sink = sink_ref[hq]
            sl = jnp.where(mask, s_loc[hq], NEG_INF)
            m = jnp.maximum(jnp.maximum(jnp.max(sl, axis=-1, keepdims=True),
                                        jnp.max(s_ctx[hq], axis=-1, keepdims=True)), sink)
            pl_h = jnp.exp(sl - m)
            pc_h = jnp.exp(s_ctx[hq] - m)
            den.append(jnp.sum(pl_h, axis=-1, keepdims=True) + jnp.sum(pc_h, axis=-1, keepdims=True)
                       + jnp.exp(sink - m))
            p_loc.append(pl_h.astype(BF16))
            p_ctx.append(pc_h.astype(BF16))
        acc = [_dot(p_loc[hq], vw[:, kvsl[hq]]) + _dot(p_ctx[hq], vc_s[:, kvsl[hq]]) for hq in heads]
        for hq in heads:
            o_ref[pl.ds(q0, SWA_BLK), hsl[hq]] = acc[hq] / den[hq]
        return carry

    lax.fori_loop(0, SEQ // SWA_BLK, latent_block, 0)

    for n in range(CTX // SWA_BLK):
        rows = slice(n * SWA_BLK, (n + 1) * SWA_BLK)
        qblk = q_s[rows, :]
        for hq in range(SWA_QH):
            kv = hq // SWA_GROUP
            hs = slice(hq * SWA_DH, (hq + 1) * SWA_DH)
            kvs = slice(kv * SWA_DH, (kv + 1) * SWA_DH)
            sink = sink_ref[hq]
            s = _dot_nt(qblk[:, hs], kc_s[:, kvs])
            m = jnp.maximum(jnp.max(s, axis=-1, keepdims=True), sink)
            p = jnp.exp(s - m)
            den = jnp.sum(p, axis=-1, keepdims=True) + jnp.exp(sink - m)
            o_ref[rows, hs] = _dot(p.astype(BF16), vc_s[:, kvs]) / den


def _swa_call(swa, qg, kg, sink, cos_t, sin_t):
    bsz = swa.shape[0]
    wq = SWA_QH * SWA_DH

    def small(shape):
        nd = len(shape)
        return pl.BlockSpec(shape, lambda b: (0,) * nd)

    return pl.pallas_call(
        _swa_kernel,
        grid=(bsz,),
        in_specs=[pl.BlockSpec(memory_space=pltpu.SMEM),
                  pl.BlockSpec((None, L, MAIN_SPLITS[1]), lambda b: (b, 0, 0)),
                  small((1, 128)), small((1, 128)), small((SEQ, 128)), small((SEQ, 128))],
        out_specs=pl.BlockSpec((None, L, wq), lambda b: (b, 0, 0)),
        out_shape=jax.ShapeDtypeStruct((bsz, L, wq), F32),
        scratch_shapes=[pltpu.VMEM((L, wq), BF16),
                        pltpu.VMEM((SEQ + 2 * SWA_BLK, 128), BF16), pltpu.VMEM((SEQ + 2 * SWA_BLK, 128), BF16),
                        pltpu.VMEM((CTX, 128), BF16), pltpu.VMEM((CTX, 128), BF16)],
        compiler_params=_cparams(("arbitrary",), 48),
        name="swa",
    )(sink, swa, qg, kg, cos_t, sin_t)


LRU_RB = 32


def _lru_kernel(in_ref, cw_ref, cb_ref, wg_ref, gb_ref, lam_ref, o_ref, u_s, a_s, b_s):
    row = _iota((L, 1), 0)
    u_s[...] = _seg_conv4(in_ref[:, 0:LRU_W], cw_ref[...], row) + cb_ref[...]

    sub = _iota((1, 8, 1), 1)
    n_blk = L // LRU_RB
    n_ctx_blk = CTX // LRU_RB

    for d in range(2):
        sp = _softplus(-lam_ref[d])

        def coeffs(t, carry, d=d, sp=sp):
            rows = pl.ds(pl.multiple_of(t * TM, TM), TM)
            ub = u_s[rows, :]
            pre = _dot(ub.astype(BF16), wg_ref[d]) + gb_ref[d]
            r = _sigmoid(pre[:, 0:LRU_W])
            i = _sigmoid(pre[:, LRU_W:])
            log_a = -LRU_C * r * sp
            a = jnp.exp(log_a)
            a_s[rows, :] = a
            b_s[rows, :] = jnp.sqrt(jnp.tanh(-log_a) * (a * a + 1.0)) * (i * ub)
            return carry

        lax.fori_loop(0, N_TILES, coeffs, 0)

        def scan_block(t, hin, d=d):
            if d == 0:
                blk = t
            else:
                blk = jnp.where(t < n_ctx_blk, n_ctx_blk - 1 - t, n_blk + n_ctx_blk - 1 - t)
            rows = pl.ds(pl.multiple_of(blk * LRU_RB, LRU_RB), LRU_RB)
            a = a_s[rows, :].reshape(LRU_RB // 8, 8, LRU_W)
            b = b_s[rows, :].reshape(LRU_RB // 8, 8, LRU_W)
            for s in (1, 2, 4):
                if d == 0:
                    valid = sub >= s
                    shift = s
                else:
                    valid = sub < 8 - s
                    shift = 8 - s
                a_sh = jnp.where(valid, pltpu.roll(a, shift, 1), 1.0)
                b_sh = jnp.where(valid, pltpu.roll(b, shift, 1), 0.0)
                b = a * b_sh + b
                a = a * a_sh
            tiles = range(LRU_RB // 8) if d == 0 else range(LRU_RB // 8 - 1, -1, -1)
            hs = [None] * (LRU_RB // 8)
            for ti in tiles:
                ht = a[ti] * hin + b[ti]
                hs[ti] = ht
                hin = ht[7:8, :] if d == 0 else ht[0:1, :]
            hblk = jnp.concatenate(hs, axis=0)
            if d == 0:
                o_ref[rows, :] = hblk
            else:
                o_ref[rows, :] += hblk
            return hin

        lax.fori_loop(0, n_blk, scan_block, jnp.zeros((1, LRU_W), F32))

    y = in_ref[:, LRU_W:]
    gelu = 0.5 * y * (1.0 + jnp.tanh(np.sqrt(2.0 / np.pi) * (y + 0.044715 * (y * y * y))))
    o_ref[...] = o_ref[...] * gelu


def _lru_call(lru, conv_w, conv_b, wg, gb, lam):
    bsz = lru.shape[0]

    def small(shape):
        nd = len(shape)
        return pl.BlockSpec(shape, lambda b: (0,) * nd)

    return pl.pallas_call(
        _lru_kernel,
        grid=(bsz,),
        in_specs=[pl.BlockSpec((None, L, 2 * LRU_W), lambda b: (b, 0, 0)),
                  small((4, LRU_W)), small((1, LRU_W)), small((2, LRU_W, 2 * LRU_W)),
                  small((2, 1, 2 * LRU_W)), small((2, 1, LRU_W))],
        out_specs=pl.BlockSpec((None, L, LRU_W), lambda b: (b, 0, 0)),
        out_shape=jax.ShapeDtypeStruct((bsz, L, LRU_W), F32),
        scratch_shapes=[pltpu.VMEM((L, LRU_W), F32)] * 3,
        compiler_params=_cparams(("arbitrary",), 56),
        name="lru",
    )(lru, conv_w, conv_b, wg, gb, lam)


def _merge_kernel(a_ref, b_ref, c_ref, gl_ref, x_ref, mod_ref, g2_ref, wb_ref, wo_ref, wr_ref,
                  xn_ref, h2_ref, aff_ref):
    mixed = None
    for i, br in enumerate((a_ref, b_ref, c_ref)):
        term = _sigmoid(gl_ref[:, i * D:(i + 1) * D]) * _dot(br[...].astype(BF16), wb_ref[i])
        mixed = term if mixed is None else mixed + term
    mod = mod_ref[...]
    xn = x_ref[...] + mod[2:3] * _dot(mixed.astype(BF16), wo_ref[...])
    xn_ref[...] = xn
    h2 = _modulated_norm(xn, g2_ref[...], mod[3:4], mod[4:5])
    h2_ref[...] = h2.astype(BF16)
    logits = _dot3(h2, wr_ref[...])
    e = jnp.exp(logits - jnp.max(logits, axis=-1, keepdims=True))
    aff_ref[...] = e / jnp.sum(e, axis=-1, keepdims=True)


def _merge_call(a_br, b_br, c_br, gate, xt, modsel, g2, wb, wo, wr, layer):
    bsz = xt.shape[0]
    return pl.pallas_call(
        _merge_kernel,
        grid=(bsz, N_TILES),
        in_specs=[_tile_spec(512), _tile_spec(512), _tile_spec(512), _tile_spec(3 * D), _tile_spec(D),
                  _mod_spec(), _const_spec((1, D)), _layer_spec((3, 512, D), layer), _layer_spec((D, D), layer),
                  _const_spec((D, N_EXP))],
        out_specs=[_tile_spec(D), _tile_spec(D), _tile_spec(N_EXP)],
        out_shape=[jax.ShapeDtypeStruct((bsz, L, D), F32), jax.ShapeDtypeStruct((bsz, L, D), BF16),
                   jax.ShapeDtypeStruct((bsz, L, N_EXP), F32)],
        compiler_params=_cparams(("arbitrary", "arbitrary"), 40),
        name="merge",
    )(a_br, b_br, c_br, gate, xt, modsel, g2, wb, wo, wr)


def _excl_prefix(m):
    w = 256
    pp = _iota((w, w), 0)
    jj = _iota((w, w), 1)
    upper = jnp.where(pp < jj, 1.0, 0.0).astype(BF16)
    off = jnp.zeros((m.shape[0], 1), F32)
    parts = []
    for c0 in range(0, m.shape[1], w):
        blk = m[:, c0:c0 + w]
        parts.append(_dot(blk.astype(BF16), upper) + off)
        off = off + jnp.sum(blk, axis=1, keepdims=True)
    return parts[0] if len(parts) == 1 else jnp.concatenate(parts, axis=1)


def _topk_pos(v, k):
    bits = pltpu.bitcast(v, I32)
    thr = jnp.zeros((v.shape[0], 1), I32)
    for bit in range(30, -1, -1):
        cand = thr | (1 << bit)
        cnt = jnp.sum(jnp.where(bits >= cand, 1.0, 0.0), axis=1, keepdims=True)
        thr = jnp.where(cnt >= k, cand, thr)
    gt = bits > thr
    eq = jnp.where(bits == thr, 1.0, 0.0)
    need = k - jnp.sum(jnp.where(gt, 1.0, 0.0), axis=1, keepdims=True)
    sel = jnp.where(gt, 1.0, jnp.where(_excl_prefix(eq) < need, eq, 0.0))
    return jnp.where(sel > 0.0, _excl_prefix(sel), -1.0).astype(I32)


def _select_kernel(aff_ref, pos_ref):
    pos_ref[:, 0:CTX] = _topk_pos(aff_ref[:, 0:CTX], CAP_C)
    pos_ref[:, CTX:L] = _topk_pos(aff_ref[:, CTX:L], CAP_X)


def _select_call(aff_t):
    bsz = aff_t.shape[0]
    return pl.pallas_call(
        _select_kernel,
        grid=(bsz,),
        in_specs=[pl.BlockSpec((None, N_EXP, L), lambda b: (b, 0, 0))],
        out_specs=pl.BlockSpec((None, N_EXP, L), lambda b: (b, 0, 0)),
        out_shape=jax.ShapeDtypeStruct((bsz, N_EXP, L), I32),
        compiler_params=_cparams(("arbitrary",), 32),
        name="select",
    )(aff_t)


GATHER_EXPERTS = 4


def _gather_kernel(pos_ref, aff_ref, h_ref, xs_ref, xc_ref, gx_ref, gc_ref):
    for lo, hi, cap, x_ref, g_ref in ((CTX, L, CAP_X, xs_ref, gx_ref), (0, CTX, CAP_C, xc_ref, gc_ref)):
        ranks = _iota((cap, hi - lo), 0)
        hits = [ranks == pos_ref[g][:, lo:hi] for g in range(GATHER_EXPERTS)]
        onehot = jnp.concatenate([jnp.where(hit, 1.0, 0.0).astype(BF16) for hit in hits], axis=0)
        rows = _dot(onehot, h_ref[lo:hi, :]).astype(BF16)
        for g in range(GATHER_EXPERTS):
            x_ref[g] = rows[g * cap:(g + 1) * cap, :]
            g_ref[g] = jnp.sum(jnp.where(hits[g], aff_ref[g][:, lo:hi], 0.0), axis=1, keepdims=True)


def _gather_call(pos4, aff4, h2):
    bsz = h2.shape[0]
    row_spec = pl.BlockSpec((None, GATHER_EXPERTS, 1, L), lambda b, e: (b, e, 0, 0))

    def cap_spec(cap, width):
        return pl.BlockSpec((GATHER_EXPERTS, cap, width), lambda b, e: (e, b, 0))

    return pl.pallas_call(
        _gather_kernel,
        grid=(bsz, N_EXP // GATHER_EXPERTS),
        in_specs=[row_spec, row_spec, pl.BlockSpec((None, L, D), lambda b, e: (b, 0, 0))],
        out_specs=[cap_spec(CAP_X, D), cap_spec(CAP_C, D), cap_spec(CAP_X, 1), cap_spec(CAP_C, 1)],
        out_shape=[jax.ShapeDtypeStruct((N_EXP, bsz * CAP_X, D), BF16),
                   jax.ShapeDtypeStruct((N_EXP, bsz * CAP_C, D), BF16),
                   jax.ShapeDtypeStruct((N_EXP, bsz * CAP_X, 1), F32),
                   jax.ShapeDtypeStruct((N_EXP, bsz * CAP_C, 1), F32)],
        compiler_params=_cparams(("arbitrary", "arbitrary"), 48),
        name="gather",
    )(pos4, aff4, h2)


FFN_HC = 512
FFN_MT = 512


def _ffn_kernel(xs_ref, xc_ref, gx_ref, gc_ref, wg_ref, wu_ref, wd_ref, ys_ref, yc_ref,
                acc_x, acc_c, wgb, wub, wdb, *, need_ctx):
    hc = pl.program_id(1)
    wgb[...] = wg_ref[...].astype(BF16)
    wub[...] = wu_ref[...].astype(BF16)
    wdb[...] = wd_ref[...].astype(BF16)

    def ffn(x):
        g = _dot(x, wgb[...])
        hid = _silu(g) * _dot(x, wub[...])
        return _dot(hid.astype(BF16), wdb[...])

    @pl.when(hc == 0)
    def _():
        acc_x[...] = jnp.zeros_like(acc_x)
        acc_c[...] = jnp.zeros_like(acc_c)

    for m0 in range(0, xs_ref.shape[0], FFN_MT):
        acc_x[m0:m0 + FFN_MT, :] += ffn(xs_ref[m0:m0 + FFN_MT, :])
    if need_ctx:
        acc_c[...] += ffn(xc_ref[...])

    @pl.when(hc == pl.num_programs(1) - 1)
    def _():
        ys_ref[...] = (acc_x[...] * gx_ref[...]).astype(BF16)
        yc_ref[...] = (acc_c[...] * gc_ref[...]).astype(BF16)


def _ffn_call(xs, xc, gx, gc, w_gate, w_up, w_down, layer, need_ctx):
    mx, mc = xs.shape[1], xc.shape[1]
    return pl.pallas_call(
        functools.partial(_ffn_kernel, need_ctx=need_ctx),
        grid=(N_EXP, EXP_HID // FFN_HC),
        in_specs=[pl.BlockSpec((None, mx, D), lambda e, j: (e, 0, 0)),
                  pl.BlockSpec((None, mc, D), lambda e, j: (e, 0, 0)),
                  pl.BlockSpec((None, mx, 1), lambda e, j: (e, 0, 0)),
                  pl.BlockSpec((None, mc, 1), lambda e, j: (e, 0, 0)),
                  pl.BlockSpec((None, None, D, FFN_HC), lambda e, j: (layer, e, 0, j)),
                  pl.BlockSpec((None, None, D, FFN_HC), lambda e, j: (layer, e, 0, j)),
                  pl.BlockSpec((None, None, FFN_HC, D), lambda e, j: (layer, e, j, 0))],
        out_specs=[pl.BlockSpec((None, mx, D), lambda e, j: (e, 0, 0)),
                   pl.BlockSpec((None, mc, D), lambda e, j: (e, 0, 0))],
        out_shape=[jax.ShapeDtypeStruct(xs.shape, BF16), jax.ShapeDtypeStruct(xc.shape, BF16)],
        scratch_shapes=[pltpu.VMEM((mx, D), F32), pltpu.VMEM((mc, D), F32),
                        pltpu.VMEM((D, FFN_HC), BF16), pltpu.VMEM((D, FFN_HC), BF16),
                        pltpu.VMEM((FFN_HC, D), BF16)],
        compiler_params=_cparams(("arbitrary", "arbitrary"), 56),
        name="ffn",
    )(xs, xc, gx, gc, w_gate, w_up, w_down)


SCATTER_TM = 768


def _scatter_kernel(ys_ref, yc_ref, pos_ref, x_ref, g2_ref, o_ref, *, need_ctx):
    t = pl.program_id(1)

    def scattered(r0, r1, y_ref, cap):
        pos = pos_ref[r0:r1, :]
        ranks = _iota((r1 - r0, cap), 1)
        onehot = jnp.concatenate(
            [jnp.where(pos[:, e:e + 1] == ranks, 1.0, 0.0).astype(BF16) for e in range(N_EXP)], axis=1)
        return _dot(onehot, y_ref[...].reshape(N_EXP * cap, D))

    @pl.when(t == 0)
    def _():
        if need_ctx:
            o_ref[0:CTX, :] = x_ref[0:CTX, :] + g2_ref[0] * scattered(0, CTX, yc_ref, CAP_C)
        else:
            o_ref[0:CTX, :] = x_ref[0:CTX, :]
        o_ref[CTX:, :] = x_ref[CTX:, :] + g2_ref[1] * scattered(CTX, SCATTER_TM, ys_ref, CAP_X)

    @pl.when(t > 0)
    def _():
        o_ref[...] = x_ref[...] + g2_ref[1] * scattered(0, SCATTER_TM, ys_ref, CAP_X)


def _scatter_call(ys, yc, pos_t, xn, gate2, need_ctx):
    bsz = xn.shape[0]

    def rows_spec(width):
        return pl.BlockSpec((None, SCATTER_TM, width), lambda b, t: (b, t, 0))

    return pl.pallas_call(
        functools.partial(_scatter_kernel, need_ctx=need_ctx),
        grid=(bsz, L // SCATTER_TM),
        in_specs=[pl.BlockSpec((N_EXP, CAP_X, D), lambda b, t: (0, b, 0)),
                  pl.BlockSpec((N_EXP, CAP_C, D), lambda b, t: (0, b, 0)),
                  rows_spec(N_EXP), rows_spec(D),
                  pl.BlockSpec((None, 2, 1, D), lambda b, t: (b, 0, 0, 0))],
        out_specs=rows_spec(D),
        out_shape=jax.ShapeDtypeStruct((bsz, L, D), F32),
        compiler_params=_cparams(("arbitrary", "arbitrary"), 56),
        name="scatter",
    )(ys, yc, pos_t, xn, gate2)


def _rope_tables():
    rows = SEQ // GRID_W
    row = np.repeat(np.arange(rows, dtype=np.float32), GRID_W)
    col = np.tile(np.arange(GRID_W, dtype=np.float32), rows)
    axis_dim = SWA_DH // 2
    inv_freq = (np.float32(ROPE_THETA) ** (-np.arange(0, axis_dim, 2, dtype=np.float32) / np.float32(axis_dim)))
    inv_freq = inv_freq.astype(np.float32)
    ang = np.concatenate([row[:, None] * inv_freq, col[:, None] * inv_freq], axis=-1).astype(np.float32)
    ang = np.tile(np.repeat(ang, 2, axis=-1), (1, 128 // SWA_DH))
    return jnp.asarray(np.cos(ang), F32), jnp.asarray(np.sin(ang), F32)


def _block_diag(w):
    n, bw, _ = w.shape
    eye = jnp.eye(n, dtype=w.dtype)
    return (eye[:, None, :, None] * w[:, :, None, :]).reshape(n * bw, n * bw)


def _pad16(v):
    return jnp.concatenate([v.reshape(-1), jnp.zeros((16 - v.size,), v.dtype)])


def _layer(xt, mods_l, layer, norm1_g, norm2_g, w_gdn, w_rest, w_ab, gdn_conv_w, gdn_a_log, gdn_dt_bias, gdn_onorm_g,
           swa_qnorm_g, swa_knorm_g, swa_sink, lru_conv_w, lru_conv_b, lru_gate_w, lru_gate_b, lru_lambda,
           w_branch, w_out, w_router, w_exp_gate, w_exp_up, w_exp_down, cos_t, sin_t, need_ctx):
    bsz = xt.shape[0]
    mods6 = mods_l.reshape(16, 6, D)
    modsel = jnp.stack([jnp.broadcast_to(mods6[8], (bsz, 6, D)), mods6[:bsz]], axis=1)

    gdn, swa, lru, gate, ab = _inproj_call(xt, modsel, norm1_g.reshape(1, D), w_gdn, w_rest, w_ab, layer)

    abt = jnp.swapaxes(ab.reshape(bsz, N_CHUNK, GDN_C, 16), 2, 3)
    alog16, dt16 = _pad16(gdn_a_log), _pad16(gdn_dt_bias)
    a_br = _gdn_call(gdn, ab, abt, gdn_conv_w, alog16.reshape(1, 16), dt16.reshape(1, 16),
                     alog16.reshape(16, 1), dt16.reshape(16, 1), gdn_onorm_g.reshape(1, GDN_DK))

    b_br = _swa_call(swa, jnp.tile(swa_qnorm_g, 128 // SWA_DH).reshape(1, 128),
                     jnp.tile(swa_knorm_g, 128 // SWA_DH).reshape(1, 128), swa_sink, cos_t, sin_t)

    wg = jnp.stack([jnp.concatenate([_block_diag(lru_gate_w[d, 0]), _block_diag(lru_gate_w[d, 1])], axis=1)
                    for d in range(2)]).astype(BF16)
    c_br = _lru_call(lru, lru_conv_w, lru_conv_b.reshape(1, LRU_W), wg,
                     lru_gate_b.reshape(2, 1, 2 * LRU_W), lru_lambda.reshape(2, 1, LRU_W))

    xn, h2, aff = _merge_call(a_br, b_br, c_br, gate, xt, modsel, norm2_g.reshape(1, D),
                              w_branch, w_out, w_router, layer)

    aff_t = jnp.swapaxes(aff, 1, 2)
    pos = _select_call(aff_t)
    xs, xc, gx, gc = _gather_call(pos.reshape(bsz, N_EXP, 1, L), aff_t.reshape(bsz, N_EXP, 1, L), h2)
    ys, yc = _ffn_call(xs, xc, gx, gc, w_exp_gate, w_exp_up, w_exp_down, layer, need_ctx)
    gate2 = modsel[:, :, 5:6, :]
    return _scatter_call(ys, yc, jnp.swapaxes(pos, 1, 2), xn, gate2, need_ctx)


def kernel(x, c, ctx, c_ctx, w_ada, b_ada, norm1_g, norm2_g, w_in, gdn_conv_w, gdn_a_log, gdn_dt_bias, gdn_onorm_g, swa_qnorm_g, swa_knorm_g, swa_sink, lru_conv_w, lru_conv_b, lru_gate_w, lru_gate_b, lru_lambda, w_branch, w_out, w_router, w_exp_gate, w_exp_up, w_exp_down):
    bsz = x.shape[0]
    depth = w_ada.shape[0]
    assert x.shape == (bsz, SEQ, D) and ctx.shape == (bsz, CTX, D) and bsz <= 8
    cond16 = jnp.concatenate([c, jnp.zeros((8 - bsz, D), F32), c_ctx[None, :], jnp.zeros((7, D), F32)], axis=0)
    mods = _ada_call(cond16, w_ada, b_ada)
    cos_t, sin_t = _rope_tables()
    xt = jnp.concatenate([ctx, x], axis=1)
    w_gdn = w_in[:, :, :2048].astype(BF16)
    w_rest = w_in[:, :, 2064:].astype(BF16)
    w_ab = w_in[:, :, 2048:2064].astype(BF16)
    w_branch, w_out = w_branch.astype(BF16), w_out.astype(BF16)
    for l in range(depth):
        xt = _layer(xt, mods[l], l, norm1_g[l], norm2_g[l], w_gdn, w_rest, w_ab, gdn_conv_w[l], gdn_a_log[l],
                    gdn_dt_bias[l], gdn_onorm_g[l], swa_qnorm_g[l], swa_knorm_g[l], swa_sink[l],
                    lru_conv_w[l], lru_conv_b[l], lru_gate_w[l], lru_gate_b[l], lru_lambda[l],
                    w_branch, w_out, w_router[l], w_exp_gate, w_exp_up, w_exp_down, cos_t, sin_t,
                    need_ctx=l < depth - 1)
    return xt[:, CTX:, :]
```

```python
import functools

import jax
import jax.numpy as jnp
import numpy as np
from jax import lax
from jax.experimental import pallas as pl
from jax.experimental.pallas import tpu as pltpu

F32 = jnp.float32
BF16 = jnp.bfloat16
I32 = jnp.int32

D = 1024
SEQ = 2048
CTX = 256
L = CTX + SEQ
GRID_W = 64
EPS = 1e-6
NEG_INF = -1e30
ROPE_THETA = 10000.0

GDN_H = 4
GDN_DK = 128
GDN_C = 64
N_CHUNK = L // GDN_C
N_CTX_CHUNK = CTX // GDN_C

SWA_QH = 8
SWA_KVH = 2
SWA_GROUP = SWA_QH // SWA_KVH
SWA_DH = 64
SWA_BLK = 128

LRU_W = 512
LRU_NBLK = 8
LRU_C = 8.0

N_EXP = 16
EXP_HID = 2048
CAP_X = 2 * SEQ // N_EXP
CAP_C = 2 * CTX // N_EXP

TM = 256
N_TILES = L // TM
MAIN_SPLITS = (2048, 768, 1024, 3072)
N_MAIN = sum(MAIN_SPLITS)

VMEM_MIB_V7X = 64


def _cparams(sem, vmem_mib):
    assert vmem_mib < VMEM_MIB_V7X
    return pltpu.CompilerParams(dimension_semantics=sem, vmem_limit_bytes=vmem_mib * 1024 * 1024)


def _dot(a, b):
    return jnp.dot(a, b, preferred_element_type=F32)


def _dot_nt(a, b):
    return lax.dot_general(a, b, (((1,), (1,)), ((), ())), preferred_element_type=F32)


def _dot_tn(a, b):
    return lax.dot_general(a, b, (((0,), (0,)), ((), ())), preferred_element_type=F32)


def _split(x):
    hi = x.astype(BF16)
    lo = (x - hi.astype(F32)).astype(BF16)
    return hi, lo


def _dot3(a, b):
    ah, al = _split(a)
    bh, bl = _split(b)
    return _dot(ah, bh) + _dot(al, bh) + _dot(ah, bl)


def _sigmoid(x):
    return 0.5 * jnp.tanh(0.5 * x) + 0.5


def _silu(x):
    return x * _sigmoid(x)


def _softplus(x):
    return jnp.maximum(x, 0.0) + jnp.log1p(jnp.exp(-jnp.abs(x)))


def _iota(shape, dim):
    return lax.broadcasted_iota(I32, shape, dim)


def _seg_conv4(x, w, row):
    n = x.shape[0]
    segpos = jnp.where(row < CTX, row, row - CTX)
    is_last = (row == CTX - 1) | (row == n - 1)
    xm2 = jnp.where(segpos >= 2, pltpu.roll(x, 2, 0), 0.0)
    xm1 = jnp.where(segpos >= 1, pltpu.roll(x, 1, 0), 0.0)
    xp1 = jnp.where(is_last, 0.0, pltpu.roll(x, n - 1, 0))
    return xm2 * w[0:1] + xm1 * w[1:2] + x * w[2:3] + xp1 * w[3:4]


def _ada_kernel(c_ref, w_ref, b_ref, o_ref):
    o_ref[...] = _dot3(_silu(c_ref[...]), w_ref[...]) + b_ref[...]


def _ada_call(cond16, w_ada, b_ada):
    depth, _, n = w_ada.shape
    tn = 1536
    return pl.pallas_call(
        _ada_kernel,
        grid=(depth, n // tn),
        in_specs=[pl.BlockSpec((16, D), lambda l, j: (0, 0)),
                  pl.BlockSpec((None, D, tn), lambda l, j: (l, 0, j)),
                  pl.BlockSpec((None, 1, tn), lambda l, j: (l, 0, j))],
        out_specs=pl.BlockSpec((None, 16, tn), lambda l, j: (l, 0, j)),
        out_shape=jax.ShapeDtypeStruct((depth, 16, n), F32),
        compiler_params=_cparams(("arbitrary", "arbitrary"), 32),
        name="ada",
    )(cond16, w_ada, b_ada.reshape(depth, 1, n))


def _modulated_norm(x, g, shift, scale):
    xn = x * lax.rsqrt(jnp.mean(x * x, axis=-1, keepdims=True) + EPS) * g
    return xn * (1.0 + scale) + shift


def _inproj_kernel(x_ref, mod_ref, g_ref, wgdn_ref, wrest_ref, wab_ref, gdn_ref, swa_ref, lru_ref, gate_ref, ab_ref):
    mod = mod_ref[...]
    hb = _modulated_norm(x_ref[...], g_ref[...], mod[0:1], mod[1:2]).astype(BF16)
    col = 0
    for ref, width in zip((gdn_ref, swa_ref, lru_ref, gate_ref), MAIN_SPLITS):
        w_ref = wgdn_ref if ref is gdn_ref else wrest_ref
        for c0 in range(0, width, 512):
            cw = min(512, width - c0)
            ref[:, c0:c0 + cw] = _dot(hb, w_ref[:, col + c0:col + c0 + cw])
        col = 0 if ref is gdn_ref else col + width
    ab_ref[...] = _dot(hb, wab_ref[...])


def _mod_spec():
    return pl.BlockSpec((None, None, 6, D), lambda b, t: (b, jnp.minimum(t, 1), 0, 0))


def _tile_spec(width):
    return pl.BlockSpec((None, TM, width), lambda b, t: (b, t, 0))


def _const_spec(shape):
    nd = len(shape)
    return pl.BlockSpec(shape, lambda b, t: (0,) * nd, pipeline_mode=pl.Buffered(1))


def _layer_spec(shape, layer):
    nd = len(shape)
    return pl.BlockSpec((None,) + shape, lambda b, t: (layer,) + (0,) * nd, pipeline_mode=pl.Buffered(1))


def _inproj_call(xt, modsel, g1, w_gdn, w_rest, w_ab, layer):
    bsz = xt.shape[0]
    outs = [jax.ShapeDtypeStruct((bsz, L, w), F32) for w in MAIN_SPLITS] + [jax.ShapeDtypeStruct((bsz, L, 16), F32)]
    return pl.pallas_call(
        _inproj_kernel,
        grid=(bsz, N_TILES),
        in_specs=[_tile_spec(D), _mod_spec(), _const_spec((1, D)),
                  _layer_spec((D, MAIN_SPLITS[0]), layer), _layer_spec((D, N_MAIN - MAIN_SPLITS[0]), layer),
                  _layer_spec((D, 16), layer)],
        out_specs=[_tile_spec(w) for w in MAIN_SPLITS] + [_tile_spec(16)],
        out_shape=outs,
        compiler_params=_cparams(("arbitrary", "arbitrary"), 48),
        name="inproj",
    )(xt, modsel, g1, w_gdn, w_rest, w_ab)


GDN_HPS = 2
GDN_PREP_CHUNKS = 9


def _gdn_kernel(q_ref, k_ref, v_ref, z_ref, cwq_ref, cwk_ref, cwv_ref, ab_ref, abt_ref,
                alog_r_ref, dt_r_ref, alog_c_ref, dt_c_ref, og_ref, o_ref,
                qs, ks, vs, kq_s, n_s, eg_s, orev_s):
    head0 = pl.program_id(1) * GDN_HPS
    row = _iota((L, 1), 0)

    for hh in range(GDN_HPS):
        hc = slice(hh * GDN_DK, (hh + 1) * GDN_DK)
        q = _silu(_seg_conv4(q_ref[:, hc], cwq_ref[:, hc], row))
        qs[hh] = q * lax.rsqrt(jnp.sum(q * q, axis=-1, keepdims=True) + EPS) * (GDN_DK ** -0.5)
        k = _silu(_seg_conv4(k_ref[:, hc], cwk_ref[:, hc], row))
        ks[hh] = k * lax.rsqrt(jnp.sum(k * k, axis=-1, keepdims=True) + EPS)
        vs[hh] = _silu(_seg_conv4(v_ref[:, hc], cwv_ref[:, hc], row))

    c = GDN_C
    ii = _iota((c, c), 0)
    jj = _iota((c, c), 1)
    eye = jnp.where(ii == jj, 1.0, 0.0)
    blk = [(ii // b) == (jj // b) for b in (8, 16, 32, 64)]
    lane16 = _iota((c, 16), 1)
    sub16 = _iota((16, c), 0)

    def obuf(hh, d):
        return o_ref.at[:, hh * GDN_DK:(hh + 1) * GDN_DK] if d == 0 else orev_s.at[hh]

    def prep(it, carry):
        chunks = []
        for u in range(GDN_PREP_CHUNKS):
            ci = it * GDN_PREP_CHUNKS + u
            rows = pl.ds(pl.multiple_of(ci * c, c), c)
            ab = ab_ref[rows, :]
            g_all = -jnp.exp(alog_r_ref[...]) * _softplus(ab + dt_r_ref[...])
            beta_all = _sigmoid(ab)
            g_all_t = -jnp.exp(alog_c_ref[...]) * _softplus(abt_ref[ci] + dt_c_ref[...])
            for hh in range(GDN_HPS):
                qc, kc, vc = qs[hh, rows, :], ks[hh, rows, :], vs[hh, rows, :]
                chunks.append(dict(ci=ci, rows=rows, hh=hh, qc=qc, kc=kc, vc=vc, g_all=g_all, beta_all=beta_all,
                                   g_all_t=g_all_t, kcb=kc.astype(BF16), qcb=qc.astype(BF16)))
        for ch in chunks:
            ch["kk"] = _dot_nt(ch["kcb"], ch["kcb"])
        for ch in chunks:
            ch["qk"] = _dot_nt(ch["qcb"], ch["kcb"])
        chains = []
        for ch in chunks:
            g_all, beta_all, g_all_t = ch["g_all"], ch["beta_all"], ch["g_all_t"]
            for d in range(2):
                if d == 0:
                    incl, strict, incl_t = jj <= ii, jj < ii, ii <= jj
                else:
                    incl, strict, incl_t = jj >= ii, jj > ii, ii >= jj
                a_idx = d * GDN_H + head0 + ch["hh"]
                gcol = jnp.sum(jnp.where(lane16 == a_idx, g_all, 0.0), axis=1, keepdims=True)
                bcol = jnp.sum(jnp.where(lane16 == 2 * GDN_H + a_idx, beta_all, 0.0), axis=1, keepdims=True)
                grow = jnp.sum(jnp.where(sub16 == a_idx, g_all_t, 0.0), axis=0, keepdims=True)
                gbc = jnp.sum(jnp.where(incl, grow, 0.0), axis=1, keepdims=True)
                gbr = jnp.sum(jnp.where(incl_t, gcol, 0.0), axis=0, keepdims=True)
                decay = jnp.where(incl, jnp.exp(jnp.where(incl, gbc - gbr, 0.0)), 0.0)
                a_mat = jnp.where(strict, ch["kk"] * bcol * decay, 0.0)
                egbc = jnp.exp(gbc)
                glast = gbc[c - 1:c, :] if d == 0 else gbc[0:1, :]
                chains.append(dict(
                    ch=ch, d=d, a=a_mat, egbc=egbc, eglast=jnp.exp(glast),
                    rhs=jnp.concatenate([ch["vc"] * bcol, ch["kc"] * (bcol * egbc)], axis=1).astype(BF16),
                    kd=(ch["kc"] * jnp.exp(glast - gbc)).astype(BF16),
                    qkd=(ch["qk"] * decay).astype(BF16)))
        for cn in chains:
            a8 = jnp.where(blk[0], cn["a"], 0.0)
            a8b = a8.astype(BF16)
            cn["p2"] = _dot(a8b, a8b)
            cn["ima"] = (eye - a8).astype(BF16)
        for cn in chains:
            p2b = cn["p2"].astype(BF16)
            cn["p4"] = _dot(p2b, p2b)
            cn["inv"] = _dot(cn["ima"], (eye + cn["p2"]).astype(BF16))
        for cn in chains:
            cn["inv"] = _dot(cn["inv"].astype(BF16), (eye + cn["p4"]).astype(BF16))
        for lvl in range(1, len(blk)):
            for cn in chains:
                off = jnp.where(blk[lvl] & jnp.logical_not(blk[lvl - 1]), cn["a"], 0.0)
                cn["ot"] = _dot(off.astype(BF16), cn["inv"].astype(BF16))
            for cn in chains:
                cn["inv"] = cn["inv"] - _dot(cn["inv"].astype(BF16), cn["ot"].astype(BF16))
        for cn in chains:
            cn["uw"] = _dot(cn["inv"].astype(BF16), cn["rhs"]).astype(BF16)
        for cn in chains:
            cn["kn"] = _dot_tn(cn["kd"], cn["uw"])
        for cn in chains:
            cn["qo"] = _dot(cn["qkd"], cn["uw"])
        for cn in chains:
            d, ci, rows, hh = cn["d"], cn["ch"]["ci"], cn["ch"]["rows"], cn["ch"]["hh"]
            kq_s[hh, d, ci, 0:GDN_DK, :] = cn["kn"][:, GDN_DK:].astype(BF16)
            kq_s[hh, d, ci, GDN_DK:, :] = (cn["ch"]["qc"] * cn["egbc"] - cn["qo"][:, GDN_DK:]).astype(BF16)
            n_s[hh, d, ci] = cn["kn"][:, 0:GDN_DK]
            obuf(hh, d)[rows, :] = cn["qo"][:, 0:GDN_DK]
            eg_s[hh, d, ci] = jnp.broadcast_to(cn["eglast"], (8, GDN_DK))
        return carry

    lax.fori_loop(0, N_CHUNK // GDN_PREP_CHUNKS, prep, 0)

    def step(t, states):
        ci_rev = jnp.where(t < N_CTX_CHUNK, N_CTX_CHUNK - 1 - t, N_CHUNK + N_CTX_CHUNK - 1 - t)
        cis = (t, ci_rev)
        chains = [(hh, d) for hh in range(GDN_HPS) for d in range(2)]
        rs = [_dot(kq_s[hh, d, cis[d]], states[i].astype(BF16)) for i, (hh, d) in enumerate(chains)]
        new_states = []
        for i, (hh, d) in enumerate(chains):
            rows = pl.ds(pl.multiple_of(cis[d] * c, c), c)
            obuf(hh, d)[rows, :] += rs[i][GDN_DK:, :]
            new_states.append(states[i] * eg_s[hh, d, cis[d]][0:1, :] + n_s[hh, d, cis[d]] - rs[i][0:GDN_DK, :])
        return tuple(new_states)

    zero_state = jnp.zeros((GDN_DK, GDN_DK), F32)
    lax.fori_loop(0, N_CHUNK, step, (zero_state,) * (2 * GDN_HPS))

    for hh in range(GDN_HPS):
        hc = slice(hh * GDN_DK, (hh + 1) * GDN_DK)
        o = o_ref[:, hc] + orev_s[hh]
        y = o * lax.rsqrt(jnp.mean(o * o, axis=-1, keepdims=True) + EPS) * og_ref[...]
        o_ref[:, hc] = y * _silu(z_ref[:, hc])


def _gdn_call(gdn, ab, abt, conv_w, alog_r, dt_r, alog_c, dt_c, onorm_g):
    bsz = gdn.shape[0]

    wblk = GDN_HPS * GDN_DK
    nblk = GDN_H // GDN_HPS

    def head_spec(part, buffers=2):
        return pl.BlockSpec((None, L, wblk), lambda b, h: (b, 0, part * nblk + h),
                            pipeline_mode=pl.Buffered(buffers))

    def cw_spec(part):
        return pl.BlockSpec((4, wblk), lambda b, h: (0, part * nblk + h))

    def small(shape):
        nd = len(shape)
        return pl.BlockSpec(shape, lambda b, h: (0,) * nd)

    seq = pltpu.VMEM((GDN_HPS, L, GDN_DK), F32)
    return pl.pallas_call(
        _gdn_kernel,
        grid=(bsz, nblk),
        in_specs=[head_spec(0), head_spec(1), head_spec(2), head_spec(3, buffers=1),
                  cw_spec(0), cw_spec(1), cw_spec(2),
                  pl.BlockSpec((None, L, 16), lambda b, h: (b, 0, 0)),
                  pl.BlockSpec((None, N_CHUNK, 16, GDN_C), lambda b, h: (b, 0, 0, 0)),
                  small((1, 16)), small((1, 16)), small((16, 1)), small((16, 1)), small((1, GDN_DK))],
        out_specs=pl.BlockSpec((None, L, wblk), lambda b, h: (b, 0, h)),
        out_shape=jax.ShapeDtypeStruct((bsz, L, GDN_H * GDN_DK), F32),
        scratch_shapes=[seq, seq, seq,
                        pltpu.VMEM((GDN_HPS, 2, N_CHUNK, GDN_DK + GDN_C, GDN_DK), BF16),
                        pltpu.VMEM((GDN_HPS, 2, N_CHUNK, GDN_DK, GDN_DK), F32),
                        pltpu.VMEM((GDN_HPS, 2, N_CHUNK, 8, GDN_DK), F32), seq],
        compiler_params=_cparams(("arbitrary", "arbitrary"), 60),
        name="gdn",
    )(gdn, gdn, gdn, gdn, conv_w, conv_w, conv_w, ab, abt, alog_r, dt_r, alog_c, dt_c, onorm_g)


def _swa_kernel(sink_ref, in_ref, qg_ref, kg_ref, cos_ref, sin_ref, o_ref, q_s, kp_s, vp_s, kc_s, vc_s):
    ii = _iota((128, 128), 0)
    jj = _iota((128, 128), 1)
    head_ones = jnp.where((ii // SWA_DH) == (jj // SWA_DH), 1.0, 0.0).astype(BF16)
    even = (_iota((1, 128), 1) % 2) == 0

    def norm(t, g):
        ssq = _dot((t * t).astype(BF16), head_ones)
        return t * lax.rsqrt(ssq * (1.0 / SWA_DH) + EPS) * g

    def rope(t):
        rot = jnp.where(even, -pltpu.roll(t, 127, 1), pltpu.roll(t, 1, 1))
        return t * cos_ref[...] + rot * sin_ref[...]

    scale = SWA_DH ** -0.5
    for cg in range(SWA_QH * SWA_DH // 128):
        cols = slice(cg * 128, (cg + 1) * 128)
        q_s[0:CTX, cols] = (norm(in_ref[0:CTX, cols], qg_ref[...]) * scale).astype(BF16)
        q_s[CTX:L, cols] = (rope(norm(in_ref[CTX:L, cols], qg_ref[...])) * scale).astype(BF16)
    kcols = slice(512, 640)
    vcols = slice(640, 768)
    zeros_blk = jnp.zeros((SWA_BLK, 128), BF16)
    kc_s[...] = norm(in_ref[0:CTX, kcols], kg_ref[...]).astype(BF16)
    kp_s[0:SWA_BLK, :] = zeros_blk
    kp_s[SWA_BLK:SWA_BLK + SEQ, :] = rope(norm(in_ref[CTX:L, kcols], kg_ref[...])).astype(BF16)
    kp_s[SWA_BLK + SEQ:, :] = zeros_blk
    vc_s[...] = in_ref[0:CTX, vcols].astype(BF16)
    vp_s[0:SWA_BLK, :] = zeros_blk
    vp_s[SWA_BLK:SWA_BLK + SEQ, :] = in_ref[CTX:L, vcols].astype(BF16)
    vp_s[SWA_BLK + SEQ:, :] = zeros_blk

    a_idx = _iota((SWA_BLK, 3 * SWA_BLK), 0)
    s_idx = _iota((SWA_BLK, 3 * SWA_BLK), 1)
    in_window = jnp.abs(s_idx - SWA_BLK - a_idx) <= SWA_BLK

    def latent_block(n, carry):
        q0 = pl.multiple_of(CTX + n * SWA_BLK, SWA_BLK)
        k0 = pl.multiple_of(n * SWA_BLK, SWA_BLK)
        qblk = q_s[pl.ds(q0, SWA_BLK), :]
        kw = kp_s[pl.ds(k0, 3 * SWA_BLK), :]
        vw = vp_s[pl.ds(k0, 3 * SWA_BLK), :]
        kpos = n * SWA_BLK - SWA_BLK + s_idx
        mask = in_window & (kpos >= 0) & (kpos < SEQ)
        heads = range(SWA_QH)
        hsl = [slice(hq * SWA_DH, (hq + 1) * SWA_DH) for hq in heads]
        kvsl = [slice((hq // SWA_GROUP) * SWA_DH, (hq // SWA_GROUP + 1) * SWA_DH) for hq in heads]
        s_loc = [_dot_nt(qblk[:, hsl[hq]], kw[:, kvsl[hq]]) for hq in heads]
        s_ctx = [_dot_nt(qblk[:, hsl[hq]], kc_s[:, kvsl[hq]]) for hq in heads]
        p_loc, p_ctx, den = [], [], []
        for hq in heads:
            sink = sink_ref[hq]
            sl = jnp.where(mask, s_loc[hq], NEG_INF)
            m = jnp.maximum(jnp.maximum(jnp.max(sl, axis=-1, keepdims=True),
                                        jnp.max(s_ctx[hq], axis=-1, keepdims=True)), sink)
            pl_h = jnp.exp(sl - m)
            pc_h = jnp.exp(s_ctx[hq] - m)
            den.append(jnp.sum(pl_h, axis=-1, keepdims=True) + jnp.sum(pc_h, axis=-1, keepdims=True)
                       + jnp.exp(sink - m))
            p_loc.append(pl_h.astype(BF16))
            p_ctx.append(pc_h.astype(BF16))
        acc = [_dot(p_loc[hq], vw[:, kvsl[hq]]) + _dot(p_ctx[hq], vc_s[:, kvsl[hq]]) for hq in heads]
        for hq in heads:
            o_ref[pl.ds(q0, SWA_BLK), hsl[hq]] = acc[hq] / den[hq]
        return carry

    lax.fori_loop(0, SEQ // SWA_BLK, latent_block, 0)

    for n in range(CTX // SWA_BLK):
        rows = slice(n * SWA_BLK, (n + 1) * SWA_BLK)
        qblk = q_s[rows, :]
        for hq in range(SWA_QH):
            kv = hq // SWA_GROUP
            hs = slice(hq * SWA_DH, (hq + 1) * SWA_DH)
            kvs = slice(kv * SWA_DH, (kv + 1) * SWA_DH)
            sink = sink_ref[hq]
            s = _dot_nt(qblk[:, hs], kc_s[:, kvs])
            m = jnp.maximum(jnp.max(s, axis=-1, keepdims=True), sink)
            p = jnp.exp(s - m)
            den = jnp.sum(p, axis=-1, keepdims=True) + jnp.exp(sink - m)
            o_ref[rows, hs] = _dot(p.astype(BF16), vc_s[:, kvs]) / den


def _swa_call(swa, qg, kg, sink, cos_t, sin_t):
    bsz = swa.shape[0]
    wq = SWA_QH * SWA_DH

    def small(shape):
        nd = len(shape)
        return pl.BlockSpec(shape, lambda b: (0,) * nd)

    return pl.pallas_call(
        _swa_kernel,
        grid=(bsz,),
        in_specs=[pl.BlockSpec(memory_space=pltpu.SMEM),
                  pl.BlockSpec((None, L, MAIN_SPLITS[1]), lambda b: (b, 0, 0)),
                  small((1, 128)), small((1, 128)), small((SEQ, 128)), small((SEQ, 128))],
        out_specs=pl.BlockSpec((None, L, wq), lambda b: (b, 0, 0)),
        out_shape=jax.ShapeDtypeStruct((bsz, L, wq), F32),
        scratch_shapes=[pltpu.VMEM((L, wq), BF16),
                        pltpu.VMEM((SEQ + 2 * SWA_BLK, 128), BF16), pltpu.VMEM((SEQ + 2 * SWA_BLK, 128), BF16),
                        pltpu.VMEM((CTX, 128), BF16), pltpu.VMEM((CTX, 128), BF16)],
        compiler_params=_cparams(("arbitrary",), 48),
        name="swa",
    )(sink, swa, qg, kg, cos_t, sin_t)


LRU_RB = 32


def _lru_kernel(in_ref, cw_ref, cb_ref, wg_ref, gb_ref, lam_ref, o_ref, u_s, a_s, b_s):
    row = _iota((L, 1), 0)
    u_s[...] = _seg_conv4(in_ref[:, 0:LRU_W], cw_ref[...], row) + cb_ref[...]

    sub = _iota((1, 8, 1), 1)
    n_blk = L // LRU_RB
    n_ctx_blk = CTX // LRU_RB

    for d in range(2):
        sp = _softplus(-lam_ref[d])

        def coeffs(t, carry, d=d, sp=sp):
            rows = pl.ds(pl.multiple_of(t * TM, TM), TM)
            ub = u_s[rows, :]
            pre = _dot(ub.astype(BF16), wg_ref[d]) + gb_ref[d]
            r = _sigmoid(pre[:, 0:LRU_W])
            i = _sigmoid(pre[:, LRU_W:])
            log_a = -LRU_C * r * sp
            a = jnp.exp(log_a)
            a_s[rows, :] = a
            b_s[rows, :] = jnp.sqrt(jnp.tanh(-log_a) * (a * a + 1.0)) * (i * ub)
            return carry

        lax.fori_loop(0, N_TILES, coeffs, 0)

        def scan_block(t, hin, d=d):
            if d == 0:
                blk = t
            else:
                blk = jnp.where(t < n_ctx_blk, n_ctx_blk - 1 - t, n_blk + n_ctx_blk - 1 - t)
            rows = pl.ds(pl.multiple_of(blk * LRU_RB, LRU_RB), LRU_RB)
            a = a_s[rows, :].reshape(LRU_RB // 8, 8, LRU_W)
            b = b_s[rows, :].reshape(LRU_RB // 8, 8, LRU_W)
            for s in (1, 2, 4):
                if d == 0:
                    valid = sub >= s
                    shift = s
                else:
                    valid = sub < 8 - s
                    shift = 8 - s
                a_sh = jnp.where(valid, pltpu.roll(a, shift, 1), 1.0)
                b_sh = jnp.where(valid, pltpu.roll(b, shift, 1), 0.0)
                b = a * b_sh + b
                a = a * a_sh
            tiles = range(LRU_RB // 8) if d == 0 else range(LRU_RB // 8 - 1, -1, -1)
            hs = [None] * (LRU_RB // 8)
            for ti in tiles:
                ht = a[ti] * hin + b[ti]
                hs[ti] = ht
                hin = ht[7:8, :] if d == 0 else ht[0:1, :]
            hblk = jnp.concatenate(hs, axis=0)
            if d == 0:
                o_ref[rows, :] = hblk
            else:
                o_ref[rows, :] += hblk
            return hin

        lax.fori_loop(0, n_blk, scan_block, jnp.zeros((1, LRU_W), F32))

    y = in_ref[:, LRU_W:]
    gelu = 0.5 * y * (1.0 + jnp.tanh(np.sqrt(2.0 / np.pi) * (y + 0.044715 * (y * y * y))))
    o_ref[...] = o_ref[...] * gelu


def _lru_call(lru, conv_w, conv_b, wg, gb, lam):
    bsz = lru.shape[0]

    def small(shape):
        nd = len(shape)
        return pl.BlockSpec(shape, lambda b: (0,) * nd)

    return pl.pallas_call(
        _lru_kernel,
        grid=(bsz,),
        in_specs=[pl.BlockSpec((None, L, 2 * LRU_W), lambda b: (b, 0, 0)),
                  small((4, LRU_W)), small((1, LRU_W)), small((2, LRU_W, 2 * LRU_W)),
                  small((2, 1, 2 * LRU_W)), small((2, 1, LRU_W))],
        out_specs=pl.BlockSpec((None, L, LRU_W), lambda b: (b, 0, 0)),
        out_shape=jax.ShapeDtypeStruct((bsz, L, LRU_W), F32),
        scratch_shapes=[pltpu.VMEM((L, LRU_W), F32)] * 3,
        compiler_params=_cparams(("arbitrary",), 56),
        name="lru",
    )(lru, conv_w, conv_b, wg, gb, lam)


def _merge_kernel(a_ref, b_ref, c_ref, gl_ref, x_ref, mod_ref, g2_ref, wb_ref, wo_ref, wr_ref,
                  xn_ref, h2_ref, aff_ref):
    mixed = None
    for i, br in enumerate((a_ref, b_ref, c_ref)):
        term = _sigmoid(gl_ref[:, i * D:(i + 1) * D]) * _dot(br[...].astype(BF16), wb_ref[i])
        mixed = term if mixed is None else mixed + term
    mod = mod_ref[...]
    xn = x_ref[...] + mod[2:3] * _dot(mixed.astype(BF16), wo_ref[...])
    xn_ref[...] = xn
    h2 = _modulated_norm(xn, g2_ref[...], mod[3:4], mod[4:5])
    h2_ref[...] = h2.astype(BF16)
    h_hi, h_lo = _split(h2)
    w_hi, w_lo = _split(wr_ref[...])
    both = _dot(h_hi, jnp.concatenate([w_hi, w_lo], axis=1))
    logits = both[:, 0:N_EXP] + _dot(h_lo, w_hi) + both[:, N_EXP:]
    e = jnp.exp(logits - jnp.max(logits, axis=-1, keepdims=True))
    aff_ref[...] = e / jnp.sum(e, axis=-1, keepdims=True)


def _merge_call(a_br, b_br, c_br, gate, xt, modsel, g2, wb, wo, wr, layer):
    bsz = xt.shape[0]
    return pl.pallas_call(
        _merge_kernel,
        grid=(bsz, N_TILES),
        in_specs=[_tile_spec(512), _tile_spec(512), _tile_spec(512), _tile_spec(3 * D), _tile_spec(D),
                  _mod_spec(), _const_spec((1, D)), _layer_spec((3, 512, D), layer), _layer_spec((D, D), layer),
                  _const_spec((D, N_EXP))],
        out_specs=[_tile_spec(D), _tile_spec(D), _tile_spec(N_EXP)],
        out_shape=[jax.ShapeDtypeStruct((bsz, L, D), F32), jax.ShapeDtypeStruct((bsz, L, D), BF16),
                   jax.ShapeDtypeStruct((bsz, L, N_EXP), F32)],
        compiler_params=_cparams(("arbitrary", "arbitrary"), 40),
        name="merge",
    )(a_br, b_br, c_br, gate, xt, modsel, g2, wb, wo, wr)


def _excl_prefix(m):
    w = 256
    pp = _iota((w, w), 0)
    jj = _iota((w, w), 1)
    upper = jnp.where(pp < jj, 1.0, 0.0).astype(BF16)
    off = jnp.zeros((m.shape[0], 1), F32)
    parts = []
    for c0 in range(0, m.shape[1], w):
        blk = m[:, c0:c0 + w]
        parts.append(_dot(blk.astype(BF16), upper) + off)
        off = off + jnp.sum(blk, axis=1, keepdims=True)
    return parts[0] if len(parts) == 1 else jnp.concatenate(parts, axis=1)


def _topk_pos(v, k):
    bits = pltpu.bitcast(v, I32)
    thr = jnp.zeros((v.shape[0], 1), I32)
    for bit in range(30, -1, -1):
        cand = thr | (1 << bit)
        cnt = jnp.sum(jnp.where(bits >= cand, 1.0, 0.0), axis=1, keepdims=True)
        thr = jnp.where(cnt >= k, cand, thr)
    gt = bits > thr
    eq = jnp.where(bits == thr, 1.0, 0.0)
    need = k - jnp.sum(jnp.where(gt, 1.0, 0.0), axis=1, keepdims=True)
    sel = jnp.where(gt, 1.0, jnp.where(_excl_prefix(eq) < need, eq, 0.0))
    return jnp.where(sel > 0.0, _excl_prefix(sel), -1.0).astype(I32)


def _select_kernel(aff_ref, pos_ref):
    pos_ref[:, 0:CTX] = _topk_pos(aff_ref[:, 0:CTX], CAP_C)
    pos_ref[:, CTX:L] = _topk_pos(aff_ref[:, CTX:L], CAP_X)


def _select_call(aff_t):
    bsz = aff_t.shape[0]
    return pl.pallas_call(
        _select_kernel,
        grid=(bsz,),
        in_specs=[pl.BlockSpec((None, N_EXP, L), lambda b: (b, 0, 0))],
        out_specs=pl.BlockSpec((None, N_EXP, L), lambda b: (b, 0, 0)),
        out_shape=jax.ShapeDtypeStruct((bsz, N_EXP, L), I32),
        compiler_params=_cparams(("arbitrary",), 32),
        name="select",
    )(aff_t)


GATHER_EXPERTS = 4


def _gather_kernel(pos_ref, aff_ref, h_ref, xs_ref, xc_ref, gx_ref, gc_ref):
    for lo, hi, cap, x_ref, g_ref in ((CTX, L, CAP_X, xs_ref, gx_ref), (0, CTX, CAP_C, xc_ref, gc_ref)):
        ranks = _iota((cap, hi - lo), 0)
        hits = [ranks == pos_ref[g][:, lo:hi] for g in range(GATHER_EXPERTS)]
        onehot = jnp.concatenate([jnp.where(hit, 1.0, 0.0).astype(BF16) for hit in hits], axis=0)
        rows = _dot(onehot, h_ref[lo:hi, :]).astype(BF16)
        for g in range(GATHER_EXPERTS):
            x_ref[g] = rows[g * cap:(g + 1) * cap, :]
            g_ref[g] = jnp.sum(jnp.where(hits[g], aff_ref[g][:, lo:hi], 0.0), axis=1, keepdims=True)


def _gather_call(pos4, aff4, h2):
    bsz = h2.shape[0]
    row_spec = pl.BlockSpec((None, GATHER_EXPERTS, 1, L), lambda b, e: (b, e, 0, 0))

    def cap_spec(cap, width):
        return pl.BlockSpec((GATHER_EXPERTS, cap, width), lambda b, e: (e, b, 0))

    return pl.pallas_call(
        _gather_kernel,
        grid=(bsz, N_EXP // GATHER_EXPERTS),
        in_specs=[row_spec, row_spec, pl.BlockSpec((None, L, D), lambda b, e: (b, 0, 0))],
        out_specs=[cap_spec(CAP_X, D), cap_spec(CAP_C, D), cap_spec(CAP_X, 1), cap_spec(CAP_C, 1)],
        out_shape=[jax.ShapeDtypeStruct((N_EXP, bsz * CAP_X, D), BF16),
                   jax.ShapeDtypeStruct((N_EXP, bsz * CAP_C, D), BF16),
                   jax.ShapeDtypeStruct((N_EXP, bsz * CAP_X, 1), F32),
                   jax.ShapeDtypeStruct((N_EXP, bsz * CAP_C, 1), F32)],
        compiler_params=_cparams(("arbitrary", "arbitrary"), 48),
        name="gather",
    )(pos4, aff4, h2)


FFN_HC = 512
FFN_MT = 512


def _ffn_kernel(xs_ref, xc_ref, gx_ref, gc_ref, wg_ref, wu_ref, wd_ref, ys_ref, yc_ref,
                acc_x, acc_c, wgb, wub, wdb, *, need_ctx):
    hc = pl.program_id(1)
    wgb[...] = wg_ref[...].astype(BF16)
    wub[...] = wu_ref[...].astype(BF16)
    wdb[...] = wd_ref[...].astype(BF16)

    def ffn(x):
        g = _dot(x, wgb[...])
        hid = _silu(g) * _dot(x, wub[...])
        return _dot(hid.astype(BF16), wdb[...])

    @pl.when(hc == 0)
    def _():
        acc_x[...] = jnp.zeros_like(acc_x)
        acc_c[...] = jnp.zeros_like(acc_c)

    for m0 in range(0, xs_ref.shape[0], FFN_MT):
        acc_x[m0:m0 + FFN_MT, :] += ffn(xs_ref[m0:m0 + FFN_MT, :])
    if need_ctx:
        acc_c[...] += ffn(xc_ref[...])

    @pl.when(hc == pl.num_programs(1) - 1)
    def _():
        ys_ref[...] = (acc_x[...] * gx_ref[...]).astype(BF16)
        yc_ref[...] = (acc_c[...] * gc_ref[...]).astype(BF16)


def _ffn_call(xs, xc, gx, gc, w_gate, w_up, w_down, layer, need_ctx):
    mx, mc = xs.shape[1], xc.shape[1]
    return pl.pallas_call(
        functools.partial(_ffn_kernel, need_ctx=need_ctx),
        grid=(N_EXP, EXP_HID // FFN_HC),
        in_specs=[pl.BlockSpec((None, mx, D), lambda e, j: (e, 0, 0)),
                  pl.BlockSpec((None, mc, D), lambda e, j: (e, 0, 0)),
                  pl.BlockSpec((None, mx, 1), lambda e, j: (e, 0, 0)),
                  pl.BlockSpec((None, mc, 1), lambda e, j: (e, 0, 0)),
                  pl.BlockSpec((None, None, D, FFN_HC), lambda e, j: (layer, e, 0, j)),
                  pl.BlockSpec((None, None, D, FFN_HC), lambda e, j: (layer, e, 0, j)),
                  pl.BlockSpec((None, None, FFN_HC, D), lambda e, j: (layer, e, j, 0))],
        out_specs=[pl.BlockSpec((None, mx, D), lambda e, j: (e, 0, 0)),
                   pl.BlockSpec((None, mc, D), lambda e, j: (e, 0, 0))],
        out_shape=[jax.ShapeDtypeStruct(xs.shape, BF16), jax.ShapeDtypeStruct(xc.shape, BF16)],
        scratch_shapes=[pltpu.VMEM((mx, D), F32), pltpu.VMEM((mc, D), F32),
                        pltpu.VMEM((D, FFN_HC), BF16), pltpu.VMEM((D, FFN_HC), BF16),
                        pltpu.VMEM((FFN_HC, D), BF16)],
        compiler_params=_cparams(("arbitrary", "arbitrary"), 56),
        name="ffn",
    )(xs, xc, gx, gc, w_gate, w_up, w_down)


SCATTER_TM = 768


def _scatter_kernel(ys_ref, yc_ref, pos_ref, x_ref, g2_ref, o_ref, *, need_ctx):
    t = pl.program_id(1)

    def scattered(r0, r1, y_ref, cap):
        pos = pos_ref[r0:r1, :]
        ranks = _iota((r1 - r0, cap), 1)
        onehot = jnp.concatenate(
            [jnp.where(pos[:, e:e + 1] == ranks, 1.0, 0.0).astype(BF16) for e in range(N_EXP)], axis=1)
        return _dot(onehot, y_ref[...].reshape(N_EXP * cap, D))

    @pl.when(t == 0)
    def _():
        if need_ctx:
            o_ref[0:CTX, :] = x_ref[0:CTX, :] + g2_ref[0] * scattered(0, CTX, yc_ref, CAP_C)
        else:
            o_ref[0:CTX, :] = x_ref[0:CTX, :]
        o_ref[CTX:, :] = x_ref[CTX:, :] + g2_ref[1] * scattered(CTX, SCATTER_TM, ys_ref, CAP_X)

    @pl.when(t > 0)
    def _():
        o_ref[...] = x_ref[...] + g2_ref[1] * scattered(0, SCATTER_TM, ys_ref, CAP_X)


def _scatter_call(ys, yc, pos_t, xn, gate2, need_ctx):
    bsz = xn.shape[0]

    def rows_spec(width):
        return pl.BlockSpec((None, SCATTER_TM, width), lambda b, t: (b, t, 0))

    return pl.pallas_call(
        functools.partial(_scatter_kernel, need_ctx=need_ctx),
        grid=(bsz, L // SCATTER_TM),
        in_specs=[pl.BlockSpec((N_EXP, CAP_X, D), lambda b, t: (0, b, 0)),
                  pl.BlockSpec((N_EXP, CAP_C, D), lambda b, t: (0, b, 0)),
                  rows_spec(N_EXP), rows_spec(D),
                  pl.BlockSpec((None, 2, 1, D), lambda b, t: (b, 0, 0, 0))],
        out_specs=rows_spec(D),
        out_shape=jax.ShapeDtypeStruct((bsz, L, D), F32),
        compiler_params=_cparams(("arbitrary", "arbitrary"), 56),
        name="scatter",
    )(ys, yc, pos_t, xn, gate2)


def _rope_tables():
    rows = SEQ // GRID_W
    row = np.repeat(np.arange(rows, dtype=np.float32), GRID_W)
    col = np.tile(np.arange(GRID_W, dtype=np.float32), rows)
    axis_dim = SWA_DH // 2
    inv_freq = (np.float32(ROPE_THETA) ** (-np.arange(0, axis_dim, 2, dtype=np.float32) / np.float32(axis_dim)))
    inv_freq = inv_freq.astype(np.float32)
    ang = np.concatenate([row[:, None] * inv_freq, col[:, None] * inv_freq], axis=-1).astype(np.float32)
    ang = np.tile(np.repeat(ang, 2, axis=-1), (1, 128 // SWA_DH))
    return jnp.asarray(np.cos(ang), F32), jnp.asarray(np.sin(ang), F32)


def _block_diag(w):
    n, bw, _ = w.shape
    eye = jnp.eye(n, dtype=w.dtype)
    return (eye[:, None, :, None] * w[:, :, None, :]).reshape(n * bw, n * bw)


def _pad16(v):
    return jnp.concatenate([v.reshape(-1), jnp.zeros((16 - v.size,), v.dtype)])


def _layer(xt, mods_l, layer, norm1_g, norm2_g, w_gdn, w_rest, w_ab, gdn_conv_w, gdn_a_log, gdn_dt_bias, gdn_onorm_g,
           swa_qnorm_g, swa_knorm_g, swa_sink, lru_conv_w, lru_conv_b, lru_gate_w, lru_gate_b, lru_lambda,
           w_branch, w_out, w_router, w_exp_gate, w_exp_up, w_exp_down, cos_t, sin_t, need_ctx):
    bsz = xt.shape[0]
    mods6 = mods_l.reshape(16, 6, D)
    modsel = jnp.stack([jnp.broadcast_to(mods6[8], (bsz, 6, D)), mods6[:bsz]], axis=1)

    gdn, swa, lru, gate, ab = _inproj_call(xt, modsel, norm1_g.reshape(1, D), w_gdn, w_rest, w_ab, layer)

    abt = jnp.swapaxes(ab.reshape(bsz, N_CHUNK, GDN_C, 16), 2, 3)
    alog16, dt16 = _pad16(gdn_a_log), _pad16(gdn_dt_bias)
    a_br = _gdn_call(gdn, ab, abt, gdn_conv_w, alog16.reshape(1, 16), dt16.reshape(1, 16),
                     alog16.reshape(16, 1), dt16.reshape(16, 1), gdn_onorm_g.reshape(1, GDN_DK))

    b_br = _swa_call(swa, jnp.tile(swa_qnorm_g, 128 // SWA_DH).reshape(1, 128),
                     jnp.tile(swa_knorm_g, 128 // SWA_DH).reshape(1, 128), swa_sink, cos_t, sin_t)

    wg = jnp.stack([jnp.concatenate([_block_diag(lru_gate_w[d, 0]), _block_diag(lru_gate_w[d, 1])], axis=1)
                    for d in range(2)]).astype(BF16)
    c_br = _lru_call(lru, lru_conv_w, lru_conv_b.reshape(1, LRU_W), wg,
                     lru_gate_b.reshape(2, 1, 2 * LRU_W), lru_lambda.reshape(2, 1, LRU_W))

    xn, h2, aff = _merge_call(a_br, b_br, c_br, gate, xt, modsel, norm2_g.reshape(1, D),
                              w_branch, w_out, w_router, layer)

    aff_t = jnp.swapaxes(aff, 1, 2)
    pos = _select_call(aff_t)
    xs, xc, gx, gc = _gather_call(pos.reshape(bsz, N_EXP, 1, L), aff_t.reshape(bsz, N_EXP, 1, L), h2)
    ys, yc = _ffn_call(xs, xc, gx, gc, w_exp_gate, w_exp_up, w_exp_down, layer, need_ctx)
    gate2 = modsel[:, :, 5:6, :]
    return _scatter_call(ys, yc, jnp.swapaxes(pos, 1, 2), xn, gate2, need_ctx)


def kernel(x, c, ctx, c_ctx, w_ada, b_ada, norm1_g, norm2_g, w_in, gdn_conv_w, gdn_a_log, gdn_dt_bias, gdn_onorm_g, swa_qnorm_g, swa_knorm_g, swa_sink, lru_conv_w, lru_conv_b, lru_gate_w, lru_gate_b, lru_lambda, w_branch, w_out, w_router, w_exp_gate, w_exp_up, w_exp_down):
    bsz = x.shape[0]
    depth = w_ada.shape[0]
    assert x.shape == (bsz, SEQ, D) and ctx.shape == (bsz, CTX, D) and bsz <= 8
    cond16 = jnp.concatenate([c, jnp.zeros((8 - bsz, D), F32), c_ctx[None, :], jnp.zeros((7, D), F32)], axis=0)
    mods = _ada_call(cond16, w_ada, b_ada)
    cos_t, sin_t = _rope_tables()
    xt = jnp.concatenate([ctx, x], axis=1)
    w_gdn = w_in[:, :, :2048].astype(BF16)
    w_rest = w_in[:, :, 2064:].astype(BF16)
    w_ab = w_in[:, :, 2048:2064].astype(BF16)
    w_branch, w_out = w_branch.astype(BF16), w_out.astype(BF16)
    for l in range(depth):
        xt = _layer(xt, mods[l], l, norm1_g[l], norm2_g[l], w_gdn, w_rest, w_ab, gdn_conv_w[l], gdn_a_log[l],
                    gdn_dt_bias[l], gdn_onorm_g[l], swa_qnorm_g[l], swa_knorm_g[l], swa_sink[l],
                    lru_conv_w[l], lru_conv_b[l], lru_gate_w[l], lru_gate_b[l], lru_lambda[l],
                    w_branch, w_out, w_router[l], w_exp_gate, w_exp_up, w_exp_down, cos_t, sin_t,
                    need_ctx=l < depth - 1)
    return xt[:, CTX:, :]
```
